```python
import functools
import numpy as np
import jax
import jax.numpy as jnp
from jax import lax

D_MODEL = 1024
BATCH = 2
SEQ = 16384
DEPTH = 1
DEC_BATCH = 128
DEC_SEQ = 8
PAST_LEN = 8192
PAGE_SIZE = 128

D_CONV = D_MODEL // 2
CONV_W = 31
N_HEADS = 8
HEAD_DIM = 64
N_KV = 2
GROUP = N_HEADS // N_KV
CMP_BLOCK = 32
CMP_STRIDE = 16
CMP_HID = 256
SEL_BLOCK = 64
SEL_TOP = 16
RATIO = SEL_BLOCK // CMP_STRIDE
WINDOW = 512
Q_BLOCK = 128
Q_W = N_HEADS * HEAD_DIM
KV_W = N_KV * HEAD_DIM
PEER_HEADS = 8
N_KEYS = 128
N_EXPERTS = N_KEYS * N_KEYS
PEER_DK = 256
PEER_TOPK = 16
PEER_CHUNK = 256
ALPHA = (2.0 * DEPTH) ** 0.25
BETA = (8.0 * DEPTH) ** -0.25
LN_EPS = 1e-5
IN_SPLITS = (D_CONV, D_CONV, Q_W, KV_W, KV_W, KV_W, KV_W, KV_W, KV_W, 3 * N_HEADS, D_MODEL, D_MODEL)
IN_W = sum(IN_SPLITS)

kernel_name = 'hybrid_conv_nsa_peer_decoder_step'


def layer_norm(x, g=None, b=None):
    xf = x.astype(jnp.float32)
    xc = xf - xf.mean(-1, keepdims=True)
    y = xc * lax.rsqrt(jnp.mean(xc * xc, -1, keepdims=True) + LN_EPS)
    if g is not None:
        y = y * g.astype(jnp.float32) + b.astype(jnp.float32)
    return y.astype(x.dtype)


def masked_softmax(s, mask):
    s = jnp.where(mask, s.astype(jnp.float32), -jnp.inf)
    m = jnp.max(s, axis=-1, keepdims=True)
    e = jnp.exp(s - jnp.where(jnp.isfinite(m), m, 0.0))
    return e / jnp.maximum(jnp.sum(e, axis=-1, keepdims=True), 1e-30)


def alibi_slopes():
    h = jnp.arange(1, N_HEADS + 1, dtype=jnp.float32)
    return (2.0 ** (-8.0 * h / N_HEADS)).reshape(N_KV, GROUP)


def adaln(c, w_ada, b_ada):
    m = jax.nn.silu(c) @ w_ada + b_ada
    return jnp.split(m[:, None, :], 6, axis=-1)


def modulate(x, shift, scale):
    return layer_norm(x) * (1.0 + scale) + shift


def project_in(h, w_in, b_in):
    cuts = [int(v) for v in np.cumsum(IN_SPLITS)[:-1]]
    return jnp.split(h @ w_in + b_in, cuts, axis=-1)


def conv_module(a_ext, conv_dw, conv_dw_b, conv_ln_g, conv_ln_b, w_conv_out):
    y = lax.conv_general_dilated(a_ext, conv_dw[:, None, :], window_strides=(1,), padding='VALID',
                                 dimension_numbers=('NWC', 'WIO', 'NWC'),
                                 feature_group_count=D_CONV) + conv_dw_b
    return jax.nn.silu(layer_norm(y, conv_ln_g, conv_ln_b)) @ w_conv_out


def compress(k, pos_emb, w1, w2):
    B, T = k.shape[:2]
    n_ch = T // CMP_STRIDE
    ch = k[:, :n_ch * CMP_STRIDE].reshape(B, n_ch, CMP_STRIDE, N_KV, HEAD_DIM)
    blocks = jnp.concatenate([ch[:, :-1], ch[:, 1:]], axis=2) + pos_emb[None, None, :, None, :]
    flat = blocks.transpose(0, 1, 3, 2, 4).reshape(B, n_ch - 1, N_KV, CMP_BLOCK * HEAD_DIM)
    return jax.nn.gelu(flat @ w1) @ w2


def to_blocks(k):
    B, T = k.shape[:2]
    ns = -(-T // SEL_BLOCK)
    k = jnp.pad(k, ((0, 0), (0, ns * SEL_BLOCK - T), (0, 0), (0, 0)))
    return k.reshape(B, ns, SEL_BLOCK, N_KV, HEAD_DIM).transpose(0, 3, 1, 2, 4)


def gather_pages(cache, layer, page_table):
    pages = cache[layer, page_table]
    return pages.reshape(page_table.shape[0], page_table.shape[1] * PAGE_SIZE, 2, N_KV, HEAD_DIM)


def nsa_core(q, t_pos, kc, vc, c_pos, ksb, vsb, kw, vw, w_pos, gates):
    B, Lq = q.shape[:2]
    NC = kc.shape[1]
    NS = ksb.shape[2]
    n_top = min(SEL_TOP, NS)
    slopes = alibi_slopes()
    sl5 = slopes[None, None, :, :, None]
    scale = HEAD_DIM ** -0.5
    qg = q.reshape(B, Lq, N_KV, GROUP, HEAD_DIM)

    d_c = (t_pos[:, None] - c_pos[None, :]).astype(jnp.float32)[None, :, None, None, :]
    s_c = jnp.einsum('bqgrd,bcgd->bqgrc', qg, kc).astype(jnp.float32) * scale - sl5 * d_c
    p_c = masked_softmax(s_c, d_c >= 0)
    o_c = jnp.einsum('bqgrc,bcgd->bqgrd', p_c.astype(vc.dtype), vc)

    imp = jnp.pad(p_c.sum(3), ((0, 0), (0, 0), (0, 0), (0, RATIO * NS - NC)))
    imp = imp.reshape(B, Lq, N_KV, NS, RATIO)
    p_slc = imp.sum(-1) + jnp.pad(imp[..., :-1, -1], ((0, 0), (0, 0), (0, 0), (1, 0)))
    blk = jnp.arange(NS)[None, :]
    cur = (t_pos // SEL_BLOCK)[:, None]
    forced = (blk == 0) | (blk == cur) | (blk == cur - 1)
    p_slc = jnp.where(forced[None, :, None, :], jnp.inf,
                      jnp.where((blk <= cur)[None, :, None, :], p_slc, -jnp.inf))
    top_v, top_i = lax.top_k(p_slc, n_top)

    bi = jnp.arange(B)[:, None, None, None]
    gi = jnp.arange(N_KV)[None, None, :, None]
    ks = ksb[bi, gi, top_i]
    vs = vsb[bi, gi, top_i]
    s_pos = top_i[..., None] * SEL_BLOCK + jnp.arange(SEL_BLOCK)
    d_s = (t_pos[None, :, None, None, None] - s_pos).astype(jnp.float32)
    m_s = (d_s >= 0) & (top_v > -jnp.inf)[..., None]
    s_s = (jnp.einsum('bqgrd,bqgnkd->bqgrnk', qg, ks).astype(jnp.float32) * scale
           - slopes[None, None, :, :, None, None] * d_s[:, :, :, None])
    n_sel = n_top * SEL_BLOCK
    p_s = masked_softmax(s_s.reshape(B, Lq, N_KV, GROUP, n_sel),
                         m_s.reshape(B, Lq, N_KV, 1, n_sel))
    o_s = jnp.einsum('bqgrm,bqgmd->bqgrd', p_s.astype(vs.dtype), vs.reshape(B, Lq, N_KV, n_sel, HEAD_DIM))

    d_w = t_pos[:, None] - w_pos[None, :]
    m_w = (d_w >= 0) & (d_w < WINDOW) & (w_pos[None, :] >= 0)
    s_w = (jnp.einsum('bqgrd,bkgd->bqgrk', qg, kw).astype(jnp.float32) * scale
           - sl5 * d_w.astype(jnp.float32)[None, :, None, None, :])
    p_w = masked_softmax(s_w, m_w[None, :, None, None, :])
    o_w = jnp.einsum('bqgrk,bkgd->bqgrd', p_w.astype(vw.dtype), vw)

    g = jax.nn.sigmoid(gates).reshape(B, Lq, N_KV, GROUP, 3)
    o = g[..., 0:1] * o_c + g[..., 1:2] * o_s + g[..., 2:3] * o_w
    return o.reshape(B, Lq, Q_W)


def nsa_prompt(q, k_c, v_c, k_s, v_s, k_w, v_w, gates, cmp):
    cmp_pos, w_k1, w_k2, w_v1, w_v2 = cmp
    B, S = q.shape[:2]
    kc = compress(k_c, cmp_pos, w_k1, w_k2)
    vc = compress(v_c, cmp_pos, w_v1, w_v2)
    c_pos = jnp.arange(kc.shape[1]) * CMP_STRIDE + (CMP_BLOCK - 1)
    ksb, vsb = to_blocks(k_s), to_blocks(v_s)
    pad = ((0, 0), (WINDOW, 0), (0, 0), (0, 0))
    kwp, vwp = jnp.pad(k_w, pad), jnp.pad(v_w, pad)

    def query_block(i):
        p0 = i * Q_BLOCK
        sl = lambda t, n: lax.dynamic_slice_in_dim(t, p0, n, axis=1)
        return nsa_core(sl(q, Q_BLOCK), p0 + jnp.arange(Q_BLOCK), kc, vc, c_pos, ksb, vsb,
                        sl(kwp, WINDOW + Q_BLOCK), sl(vwp, WINDOW + Q_BLOCK),
                        p0 - WINDOW + jnp.arange(WINDOW + Q_BLOCK), sl(gates, Q_BLOCK))

    o = lax.map(query_block, jnp.arange(S // Q_BLOCK))
    o = o.transpose(1, 0, 2, 3).reshape(B, S, Q_W)
    nb = min(WINDOW, S)
    win = jnp.stack([k_w, v_w], axis=2)
    return o, jnp.stack([k_c, v_c], axis=2), jnp.stack([k_s, v_s], axis=2), win[:, S - nb:]


def nsa_sample(q, k_c, v_c, k_s, v_s, k_w, v_w, gates, cmp, cache_cmp, cache_sel, win_state, page_table, layer):
    cmp_pos, w_k1, w_k2, w_v1, w_v2 = cmp
    L = q.shape[1]
    past_c = gather_pages(cache_cmp, layer, page_table)
    past_s = gather_pages(cache_sel, layer, page_table)
    full = lambda past, new: jnp.concatenate([past, new], axis=1)
    kc = compress(full(past_c[:, :, 0], k_c), cmp_pos, w_k1, w_k2)
    vc = compress(full(past_c[:, :, 1], v_c), cmp_pos, w_v1, w_v2)
    c_pos = jnp.arange(kc.shape[1]) * CMP_STRIDE + (CMP_BLOCK - 1)
    ksb = to_blocks(full(past_s[:, :, 0], k_s))
    vsb = to_blocks(full(past_s[:, :, 1], v_s))
    nb = win_state.shape[1]
    win = full(win_state, jnp.stack([k_w, v_w], axis=2))
    t_pos = PAST_LEN + jnp.arange(L)
    w_pos = PAST_LEN - nb + jnp.arange(nb + L)
    o = nsa_core(q, t_pos, kc, vc, c_pos, ksb, vsb, win[:, :, 0], win[:, :, 1], w_pos, gates)
    return o, jnp.stack([k_c, v_c], axis=2), jnp.stack([k_s, v_s], axis=2), win[:, L:]


def peer_ffn(h, peer_wq, peer_keys, peer_u, peer_v):
    B, L, D = h.shape
    n = B * L
    pad = (-n) % PEER_CHUNK
    hf = jnp.pad(h.reshape(n, D), ((0, pad), (0, 0)))

    def chunk(hc):
        q = (hc @ peer_wq).reshape(-1, PEER_HEADS, 2, PEER_DK // 2)
        s = jnp.einsum('nhpd,hpkd->nhpk', q, peer_keys).astype(jnp.float32)
        sv, si = lax.top_k(s, PEER_TOPK)
        cand = sv[:, :, 0, :, None] + sv[:, :, 1, None, :]
        cv, ci = lax.top_k(cand.reshape(-1, PEER_HEADS, PEER_TOPK * PEER_TOPK), PEER_TOPK)
        e1 = jnp.take_along_axis(si[:, :, 0], ci // PEER_TOPK, axis=-1)
        e2 = jnp.take_along_axis(si[:, :, 1], ci % PEER_TOPK, axis=-1)
        expert = e1 * N_KEYS + e2
        g = jax.nn.softmax(cv, axis=-1)
        act = jax.nn.gelu(jnp.einsum('nhkd,nd->nhk', peer_u[expert], hc))
        return jnp.einsum('nhk,nhkd->nd', (g * act).astype(hc.dtype), peer_v[expert])

    out = lax.map(chunk, hf.reshape(-1, PEER_CHUNK, D))
    return out.reshape(-1, D)[:n].reshape(B, L, D)


def layer_step(x, c, a_prev, nsa_fn, w_ada, b_ada, w_in, b_in, conv_dw, conv_dw_b, conv_ln_g, conv_ln_b,
               w_conv_out, cmp_pos, w_cmp_k1, w_cmp_k2, w_cmp_v1, w_cmp_v2, w_nsa_out, w_out,
               ln1_g, ln1_b, peer_wq, peer_keys, peer_u, peer_v, ln2_g, ln2_b):
    B, L = x.shape[:2]
    sh1, sc1, g1, sh2, sc2, g2 = adaln(c, w_ada, b_ada)
    h = modulate(x, sh1, sc1)
    glu_a, glu_b, q, k_c, v_c, k_s, v_s, k_w, v_w, nsa_g, gate_a, gate_b = project_in(h, w_in, b_in)
    a_ext = jnp.concatenate([a_prev, glu_a * jax.nn.sigmoid(glu_b)], axis=1)
    out_a = conv_module(a_ext, conv_dw, conv_dw_b, conv_ln_g, conv_ln_b, w_conv_out)
    heads = lambda t: t.reshape(B, L, -1, HEAD_DIM)
    o, kv_cmp, kv_sel, kv_win = nsa_fn(heads(q), heads(k_c), heads(v_c), heads(k_s), heads(v_s),
                                       heads(k_w), heads(v_w), nsa_g.reshape(B, L, N_HEADS, 3),
                                       (cmp_pos, w_cmp_k1, w_cmp_k2, w_cmp_v1, w_cmp_v2))
    out_b = o @ w_nsa_out
    mix = (jax.nn.sigmoid(gate_a) * out_a + jax.nn.sigmoid(gate_b) * out_b) @ w_out
    x1 = layer_norm(ALPHA * x + g1 * mix, ln1_g, ln1_b)
    f = peer_ffn(modulate(x1, sh2, sc2), peer_wq, peer_keys, peer_u, peer_v)
    y = layer_norm(ALPHA * x1 + g2 * f, ln2_g, ln2_b)
    return y, kv_cmp, kv_sel, kv_win, a_ext[:, a_ext.shape[1] - (CONV_W - 1):]


def setup_inputs(seed: int = 0) -> dict:
    key = jax.random.key(seed)
    keys = iter(jax.random.split(key, 48))

    def nrm(shape, s):
        return jax.random.normal(next(keys), shape, jnp.float32) * s

    def gain(shape):
        return 1.0 + nrm(shape, 0.05)

    n_pages = PAST_LEN // PAGE_SIZE
    n_used = DEC_BATCH * n_pages
    n_phys = n_used + max(1, n_used // 4)
    perm = jax.random.permutation(next(keys), n_phys)
    page_table = perm[:n_used].reshape(DEC_BATCH, n_pages).astype(jnp.int32)
    win_buf = min(WINDOW, PAST_LEN)
    Lr = DEPTH
    return {
        'x_prompt': nrm((BATCH, SEQ, D_MODEL), 1.0),
        'x_sample': nrm((DEC_BATCH, DEC_SEQ, D_MODEL), 1.0),
        'cache_cmp_kv': nrm((Lr, n_phys, PAGE_SIZE, 2, N_KV, HEAD_DIM), 1.0),
        'cache_sel_kv': nrm((Lr, n_phys, PAGE_SIZE, 2, N_KV, HEAD_DIM), 1.0),
        'state_win_kv': nrm((Lr, DEC_BATCH, win_buf, 2, N_KV, HEAD_DIM), 1.0),
        'state_conv': nrm((Lr, DEC_BATCH, CONV_W - 1, D_CONV), 0.5),
        'page_table': page_table,
        'c_prompt': nrm((BATCH, D_MODEL), 1.0),
        'c_sample': nrm((DEC_BATCH, D_MODEL), 1.0),
        'w_ada': nrm((Lr, D_MODEL, 6 * D_MODEL), D_MODEL ** -0.5),
        'b_ada': nrm((Lr, 6 * D_MODEL), 0.02),
        'w_in': nrm((Lr, D_MODEL, IN_W), D_MODEL ** -0.5),
        'b_in': nrm((Lr, IN_W), 0.02),
        'conv_dw': nrm((Lr, CONV_W, D_CONV), CONV_W ** -0.5),
        'conv_dw_b': nrm((Lr, D_CONV), 0.02),
        'conv_ln_g': gain((Lr, D_CONV)),
        'conv_ln_b': nrm((Lr, D_CONV), 0.02),
        'w_conv_out': nrm((Lr, D_CONV, D_MODEL), BETA * D_CONV ** -0.5),
        'cmp_pos': nrm((Lr, CMP_BLOCK, HEAD_DIM), 0.1),
        'w_cmp_k1': nrm((Lr, CMP_BLOCK * HEAD_DIM, CMP_HID), (CMP_BLOCK * HEAD_DIM) ** -0.5),
        'w_cmp_k2': nrm((Lr, CMP_HID, HEAD_DIM), 1.5 * CMP_HID ** -0.5),
        'w_cmp_v1': nrm((Lr, CMP_BLOCK * HEAD_DIM, CMP_HID), (CMP_BLOCK * HEAD_DIM) ** -0.5),
        'w_cmp_v2': nrm((Lr, CMP_HID, HEAD_DIM), 1.5 * CMP_HID ** -0.5),
        'w_nsa_out': nrm((Lr, Q_W, D_MODEL), BETA * Q_W ** -0.5),
        'w_out': nrm((Lr, D_MODEL, D_MODEL), BETA * D_MODEL ** -0.5),
        'ln1_g': gain((Lr, D_MODEL)),
        'ln1_b': nrm((Lr, D_MODEL), 0.02),
        'peer_wq': nrm((Lr, D_MODEL, PEER_HEADS * PEER_DK), D_MODEL ** -0.5),
        'peer_keys': nrm((Lr, PEER_HEADS, 2, N_KEYS, PEER_DK // 2), (PEER_DK // 2) ** -0.5),
        'peer_u': nrm((Lr, N_EXPERTS, D_MODEL), D_MODEL ** -0.5),
        'peer_v': nrm((Lr, N_EXPERTS, D_MODEL), BETA * PEER_HEADS ** -0.5),
        'ln2_g': gain((Lr, D_MODEL)),
        'ln2_b': nrm((Lr, D_MODEL), 0.02),
    }


def reference(x_prompt, x_sample, cache_cmp_kv, cache_sel_kv, state_win_kv, state_conv, page_table,
              c_prompt, c_sample, w_ada, b_ada, w_in, b_in, conv_dw, conv_dw_b, conv_ln_g, conv_ln_b,
              w_conv_out, cmp_pos, w_cmp_k1, w_cmp_k2, w_cmp_v1, w_cmp_v2, w_nsa_out, w_out,
              ln1_g, ln1_b, peer_wq, peer_keys, peer_u, peer_v, ln2_g, ln2_b):
    yp, ys = x_prompt, x_sample
    p_cmp, p_sel, p_win, p_conv = [], [], [], []
    s_cmp, s_sel, s_win, s_conv = [], [], [], []
    for l in range(DEPTH):
        lw = (w_ada[l], b_ada[l], w_in[l], b_in[l], conv_dw[l], conv_dw_b[l], conv_ln_g[l], conv_ln_b[l],
              w_conv_out[l], cmp_pos[l], w_cmp_k1[l], w_cmp_k2[l], w_cmp_v1[l], w_cmp_v2[l], w_nsa_out[l],
              w_out[l], ln1_g[l], ln1_b[l], peer_wq[l], peer_keys[l], peer_u[l], peer_v[l], ln2_g[l], ln2_b[l])
        prev = jnp.zeros((yp.shape[0], CONV_W - 1, D_CONV), yp.dtype)
        yp, kc_p, ks_p, kw_p, cv_p = layer_step(yp, c_prompt, prev, nsa_prompt, *lw)
        nsa_s = functools.partial(nsa_sample, cache_cmp=cache_cmp_kv, cache_sel=cache_sel_kv,
                                  win_state=state_win_kv[l], page_table=page_table, layer=l)
        ys, kc_s, ks_s, kw_s, cv_s = layer_step(ys, c_sample, state_conv[l], nsa_s, *lw)
        p_cmp.append(kc_p); p_sel.append(ks_p); p_win.append(kw_p); p_conv.append(cv_p)
        s_cmp.append(kc_s); s_sel.append(ks_s); s_win.append(kw_s); s_conv.append(cv_s)
    return (yp, ys, jnp.stack(p_cmp), jnp.stack(p_sel), jnp.stack(p_win), jnp.stack(p_conv),
            jnp.stack(s_cmp), jnp.stack(s_sel), jnp.stack(s_win), jnp.stack(s_conv))
```

```python
import functools

import numpy as np
import jax
import jax.numpy as jnp
from jax import lax
from jax.experimental import pallas as pl
from jax.experimental.pallas import tpu as pltpu

F32 = jnp.float32
BF16 = jnp.bfloat16

D_MODEL = 1024
D_CONV = 512
CONV_W = 31
N_HEADS = 8
HEAD_DIM = 64
N_KV = 2
GROUP = N_HEADS // N_KV
CMP_BLOCK = 32
CMP_STRIDE = 16
CMP_HID = 256
SEL_BLOCK = 64
SEL_TOP = 16
RATIO = SEL_BLOCK // CMP_STRIDE
WINDOW = 512
PAGE_SIZE = 128
PEER_HEADS = 8
N_KEYS = 128
PEER_TOPK = 16
DEPTH = 1
ALPHA = (2.0 * DEPTH) ** 0.25
LN_EPS = 1e-5
Q_W = N_HEADS * HEAD_DIM
KV_W = N_KV * HEAD_DIM

LANE = 128
SUBLANE = 8
VMEM_LIMIT = 56 * 1024 * 1024
NEG = -1e30
BIG_I = 1 << 20

SEG_A = 2 * D_CONV
SEG_Q = N_HEADS * LANE
SEG_KV = 6 * KV_W
SEG_KVG = 6 * KV_W
SEG_G = LANE
SEG_M = 2 * D_MODEL
OFF_A = 0
OFF_Q = OFF_A + SEG_A
OFF_KV = OFF_Q + SEG_Q
OFF_KVG = OFF_KV + SEG_KV
OFF_G = OFF_KVG + SEG_KVG
OFF_M = OFF_G + SEG_G
W_EXT = OFF_M + SEG_M


def _cparams(*sem):
    return pltpu.CompilerParams(dimension_semantics=sem, vmem_limit_bytes=VMEM_LIMIT)


def _ln(x):
    mu = jnp.mean(x, axis=-1, keepdims=True)
    xc = x - mu
    return xc * lax.rsqrt(jnp.mean(xc * xc, axis=-1, keepdims=True) + LN_EPS)


def _sigmoid(x):
    return 1.0 / (1.0 + jnp.exp(-x))


def _gelu(x):
    return 0.5 * x * (1.0 + jnp.tanh(0.7978845608028654 * (x + 0.044715 * (x * x * x))))


def _dot(a, b):
    return jnp.dot(a, b, preferred_element_type=F32)


def _adaln_kernel(c_ref, w_ref, b_ref, o_ref):
    c = c_ref[...]
    s = c * _sigmoid(c)
    o_ref[...] = _dot(s.astype(BF16), w_ref[...].astype(BF16)) + b_ref[...]


def _adaln(c, w, b):
    rows, d = c.shape
    n = w.shape[1]
    tn = 1024
    return pl.pallas_call(
        _adaln_kernel,
        grid=(n // tn,),
        in_specs=[pl.BlockSpec((rows, d), lambda j: (0, 0)),
                  pl.BlockSpec((d, tn), lambda j: (0, j)),
                  pl.BlockSpec((1, tn), lambda j: (0, j))],
        out_specs=pl.BlockSpec((rows, tn), lambda j: (0, j)),
        out_shape=jax.ShapeDtypeStruct((rows, n), F32),
        compiler_params=_cparams("parallel"),
        name="adaln",
    )(c, w, b.reshape(1, n))


def _inproj_kernel(x_ref, mod_ref, w_ref, b_ref, a_ref, q_ref, kv_ref, kvg_ref, ng_ref, ga_ref, gb_ref):
    sb, lb, d = x_ref.shape
    tm = sb * lb
    mod = mod_ref[...]
    h = _ln(x_ref[...]) * (1.0 + mod[:, :, d:2 * d]) + mod[:, :, 0:d]
    hb = h.reshape(tm, d).astype(BF16)

    def seg(off, width):
        return _dot(hb, w_ref[:, off:off + width]) + b_ref[:, off:off + width]

    z = seg(OFF_A, SEG_A)
    a_ref[...] = (z[:, :D_CONV] * _sigmoid(z[:, D_CONV:])).reshape(sb, lb, D_CONV)
    q_ref[...] = seg(OFF_Q, SEG_Q).astype(BF16).reshape(sb, lb, SEG_Q)
    kv_ref[...] = seg(OFF_KV, SEG_KV).reshape(sb, lb, SEG_KV)
    kvg_ref[...] = seg(OFF_KVG, SEG_KVG).astype(BF16).reshape(sb, lb, SEG_KVG)
    ng_ref[...] = _sigmoid(seg(OFF_G, SEG_G)).reshape(sb, lb, SEG_G)
    ga_ref[...] = _sigmoid(seg(OFF_M, D_MODEL)).astype(BF16).reshape(sb, lb, D_MODEL)
    gb_ref[...] = _sigmoid(seg(OFF_M + D_MODEL, D_MODEL)).astype(BF16).reshape(sb, lb, D_MODEL)


def _token_blocks(nseq, seqlen, tm):
    if seqlen >= tm:
        return 1, tm, (nseq, seqlen // tm)
    sb = min(tm // seqlen, nseq)
    return sb, seqlen, (nseq // sb, 1)


def _inproj(x, mod, w_ext, b_ext, tm):
    nseq, seqlen, d = x.shape
    sb, lb, grid = _token_blocks(nseq, seqlen, tm)
    tok = lambda width: pl.BlockSpec((sb, lb, width), lambda s, t: (s, t, 0))
    widths = (D_CONV, SEG_Q, SEG_KV, SEG_KVG, SEG_G, D_MODEL, D_MODEL)
    dtypes = (F32, BF16, F32, BF16, F32, BF16, BF16)
    return pl.pallas_call(
        _inproj_kernel,
        grid=grid,
        in_specs=[tok(d),
                  pl.BlockSpec((sb, 1, 2 * d), lambda s, t: (s, 0, 0)),
                  pl.BlockSpec((d, W_EXT), lambda s, t: (0, 0), pipeline_mode=pl.Buffered(1)),
                  pl.BlockSpec((1, W_EXT), lambda s, t: (0, 0), pipeline_mode=pl.Buffered(1))],
        out_specs=[tok(w) for w in widths],
        out_shape=[jax.ShapeDtypeStruct((nseq, seqlen, w), dt) for w, dt in zip(widths, dtypes)],
        compiler_params=_cparams("parallel", "parallel"),
        name="inproj",
    )(x, mod, w_ext, b_ext)


def _extend_w_in(w_in, b_in):
    scale = HEAD_DIM ** -0.5
    c_q = 2 * D_CONV
    c_kv = c_q + Q_W
    c_g = c_kv + 6 * KV_W
    c_m = c_g + 3 * N_HEADS
    cols_w, cols_b = [], []

    def add(w, b):
        cols_w.append(w)
        cols_b.append(b)

    add(w_in[:, :c_q], b_in[:c_q])
    zw = jnp.zeros((w_in.shape[0], HEAD_DIM), w_in.dtype)
    zb = jnp.zeros((HEAD_DIM,), b_in.dtype)
    for h in range(N_HEADS):
        sl = slice(c_q + h * HEAD_DIM, c_q + (h + 1) * HEAD_DIM)
        add(w_in[:, sl] * scale, b_in[sl] * scale)
        add(zw, zb)
    add(w_in[:, c_kv:c_g], b_in[c_kv:c_g])
    for br in range(3):
        k0 = c_kv + br * 2 * KV_W
        v0 = k0 + KV_W
        for g in range(N_KV):
            ks = slice(k0 + g * HEAD_DIM, k0 + (g + 1) * HEAD_DIM)
            vs = slice(v0 + g * HEAD_DIM, v0 + (g + 1) * HEAD_DIM)
            add(w_in[:, ks], b_in[ks])
            add(w_in[:, vs], b_in[vs])
    add(w_in[:, c_g:c_m], b_in[c_g:c_m])
    add(jnp.zeros((w_in.shape[0], SEG_G - 3 * N_HEADS), w_in.dtype), jnp.zeros((SEG_G - 3 * N_HEADS,), b_in.dtype))
    add(w_in[:, c_m:], b_in[c_m:])
    w = jnp.concatenate(cols_w, axis=1).astype(BF16)
    b = jnp.concatenate(cols_b, axis=0).reshape(1, -1).astype(F32)
    assert w.shape[1] == W_EXT
    return w, b


CMP_ROWS = 128


def _compress_rows(xs_ref, row0, pos_ref, w1k_ref, w1v_ref, w2_ref):
    acc_k = jnp.zeros((CMP_ROWS, 2 * CMP_HID), F32)
    acc_v = jnp.zeros((CMP_ROWS, 2 * CMP_HID), F32)
    base = 2 * CMP_STRIDE * row0
    for j in range(CMP_BLOCK):
        pos = pos_ref[j:j + 1, :]
        xk = xs_ref[pl.ds(base + 2 * j, CMP_ROWS, stride=2 * CMP_STRIDE), :] + pos
        xv = xs_ref[pl.ds(base + 2 * j + 1, CMP_ROWS, stride=2 * CMP_STRIDE), :] + pos
        acc_k = acc_k + _dot(xk.astype(BF16), w1k_ref[j])
        acc_v = acc_v + _dot(xv.astype(BF16), w1v_ref[j])
    hk = _gelu(acc_k)
    hv = _gelu(acc_v)
    outs = []
    for g in range(N_KV):
        hg = jnp.concatenate([hk[:, g * CMP_HID:(g + 1) * CMP_HID], hv[:, g * CMP_HID:(g + 1) * CMP_HID]], axis=1)
        outs.append(_dot(hg.astype(BF16), w2_ref[...]))
    return jnp.concatenate(outs, axis=1)


def _compress_weights(cmp_pos, w_k1, w_k2, w_v1, w_v2):
    def bd1(w1):
        w = w1.reshape(CMP_BLOCK, HEAD_DIM, CMP_HID)
        z = jnp.zeros_like(w)
        top = jnp.concatenate([w, z], axis=2)
        bot = jnp.concatenate([z, w], axis=2)
        return jnp.concatenate([top, bot], axis=1).astype(BF16)
    zk = jnp.zeros_like(w_k2)
    w2 = jnp.concatenate([jnp.concatenate([w_k2, zk], axis=1),
                          jnp.concatenate([zk, w_v2], axis=1)], axis=0).astype(BF16)
    pos2 = jnp.tile(cmp_pos, (1, N_KV)).astype(F32)
    return pos2, bd1(w_k1), bd1(w_v1), w2


def _compress_prompt_kernel(x_ref, halo_ref, pos_ref, w1k_ref, w1v_ref, w2_ref, o_ref, xs_ref, *, n_valid):
    rows2 = x_ref.shape[1]
    xs_ref[0:rows2, :] = x_ref[0]
    xs_ref[rows2:rows2 + 2 * CMP_STRIDE, :] = halo_ref[0]
    n_rows = rows2 // (2 * CMP_STRIDE)
    first = pl.program_id(1) * n_rows
    for r in range(n_rows // CMP_ROWS):
        out = _compress_rows(xs_ref, r * CMP_ROWS, pos_ref, w1k_ref, w1v_ref, w2_ref)
        idx = first + r * CMP_ROWS + lax.broadcasted_iota(jnp.int32, out.shape, 0)
        o_ref[0, r * CMP_ROWS:(r + 1) * CMP_ROWS, :] = jnp.where(idx < n_valid, out, 0.0).astype(o_ref.dtype)


def _compress_prompt(kvc, cw):
    nb, s2, _ = kvc.shape
    s = s2 // 2
    pos2, w1k, w1v, w2 = cw
    p = min(2048, s)
    n_ch = s // CMP_STRIDE
    rows = p // CMP_STRIDE
    const = lambda shape: pl.BlockSpec(shape, lambda b, i: (0,) * len(shape), pipeline_mode=pl.Buffered(1))
    return pl.pallas_call(
        functools.partial(_compress_prompt_kernel, n_valid=n_ch - 1),
        grid=(nb, s // p),
        in_specs=[pl.BlockSpec((1, 2 * p, LANE), lambda b, i: (b, i, 0)),
                  pl.BlockSpec((1, 2 * CMP_STRIDE, LANE), lambda b, i: (b, jnp.minimum((i + 1) * rows, n_ch - 1), 0)),
                  const(pos2.shape), const(w1k.shape), const(w1v.shape), const(w2.shape)],
        out_specs=pl.BlockSpec((1, rows, 2 * KV_W), lambda b, i: (b, i, 0)),
        out_shape=jax.ShapeDtypeStruct((nb, n_ch, 2 * KV_W), BF16),
        scratch_shapes=[pltpu.VMEM((2 * p + 2 * CMP_STRIDE, LANE), F32)],
        compiler_params=_cparams("parallel", "parallel"),
        name="compress_prompt",
    )(kvc, kvc, pos2, w1k, w1v, w2)


def _page_copy(cache_ref, pt_ref, buf_ref, sem, b, p):
    rows = cache_ref.shape[1]
    return pltpu.make_async_copy(cache_ref.at[pt_ref[b, p]], buf_ref.at[pl.ds(p * rows, rows)], sem)


def _gather_pages(cache_ref, pt_ref, buf_ref, sem, b, n_pages):
    def start(p, c):
        _page_copy(cache_ref, pt_ref, buf_ref, sem, b, p).start()
        return c
    lax.fori_loop(0, n_pages, start, 0)

    def wait(p, c):
        _page_copy(cache_ref, pt_ref, buf_ref, sem, b, p).wait()
        return c
    lax.fori_loop(0, n_pages, wait, 0)


def _compress_sample_kernel(pt_ref, cache_ref, pos_ref, w1k_ref, w1v_ref, w2_ref, o_ref, xs_ref, sem, *, n_pages):
    b = pl.program_id(0)
    past = n_pages * PAGE_SIZE
    xs_ref[2 * past:2 * past + 2 * CMP_STRIDE, :] = jnp.zeros((2 * CMP_STRIDE, LANE), F32)
    _gather_pages(cache_ref, pt_ref, xs_ref, sem, b, n_pages)
    n_rows = past // CMP_STRIDE

    def step(r, c):
        row0 = pl.multiple_of(r * CMP_ROWS, CMP_ROWS)
        out = _compress_rows(xs_ref, row0, pos_ref, w1k_ref, w1v_ref, w2_ref)
        idx = row0 + lax.broadcasted_iota(jnp.int32, out.shape, 0)
        o_ref[0, pl.ds(row0, CMP_ROWS), :] = jnp.where(idx < n_rows - 1, out, 0.0).astype(o_ref.dtype)
        return c
    lax.fori_loop(0, n_rows // CMP_ROWS, step, 0)


def _compress_sample(cache, page_table, cw):
    pos2, w1k, w1v, w2 = cw
    nb, n_pages = page_table.shape
    past = n_pages * PAGE_SIZE
    const = lambda shape: pl.BlockSpec(shape, lambda b, pt: (0,) * len(shape), pipeline_mode=pl.Buffered(1))
    return pl.pallas_call(
        functools.partial(_compress_sample_kernel, n_pages=n_pages),
        grid_spec=pltpu.PrefetchScalarGridSpec(
            num_scalar_prefetch=1,
            grid=(nb,),
            in_specs=[pl.BlockSpec(memory_space=pl.ANY),
                      const(pos2.shape), const(w1k.shape), const(w1v.shape), const(w2.shape)],
            out_specs=pl.BlockSpec((1, past // CMP_STRIDE, 2 * KV_W), lambda b, pt: (b, 0, 0)),
            scratch_shapes=[pltpu.VMEM((2 * past + 2 * CMP_STRIDE, LANE), F32), pltpu.SemaphoreType.DMA(())]),
        out_shape=jax.ShapeDtypeStruct((nb, past // CMP_STRIDE, 2 * KV_W), BF16),
        compiler_params=_cparams("arbitrary"),
        name="compress_sample",
    )(page_table, cache, pos2, w1k, w1v, w2)


def _gather_sel_kernel(pt_ref, cache_ref, new_ref, o_ref, buf_ref, sem, *, n_pages):
    b = pl.program_id(0)
    past = n_pages * PAGE_SIZE
    total = o_ref.shape[2]
    n_new = new_ref.shape[1]
    _gather_pages(cache_ref, pt_ref, buf_ref, sem, b, n_pages)
    rows = 1024

    def step(r, c):
        r0 = pl.multiple_of(r * rows, rows)
        x = buf_ref[pl.ds(r0, rows), :]
        for g in range(N_KV):
            kg = x[:, g * HEAD_DIM:(g + 1) * HEAD_DIM]
            vg = x[:, KV_W + g * HEAD_DIM:KV_W + (g + 1) * HEAD_DIM]
            o_ref[0, g, pl.ds(r0, rows), :] = jnp.concatenate([kg, vg], axis=1).astype(o_ref.dtype)
        return c
    lax.fori_loop(0, past // rows, step, 0)
    tail = jnp.zeros((total - past, LANE), o_ref.dtype)
    for g in range(N_KV):
        o_ref[0, g, past:total, :] = tail
        o_ref[0, g, past:past + n_new, :] = new_ref[0, :, g * LANE:(g + 1) * LANE]


def _gather_sel(cache, page_table, new_kvg, total):
    nb, n_pages = page_table.shape
    past = n_pages * PAGE_SIZE
    n_new = new_kvg.shape[1]
    return pl.pallas_call(
        functools.partial(_gather_sel_kernel, n_pages=n_pages),
        grid_spec=pltpu.PrefetchScalarGridSpec(
            num_scalar_prefetch=1,
            grid=(nb,),
            in_specs=[pl.BlockSpec(memory_space=pl.ANY),
                      pl.BlockSpec((1, n_new, 2 * LANE), lambda b, pt: (b, 0, 0))],
            out_specs=pl.BlockSpec((1, N_KV, total, LANE), lambda b, pt: (b, 0, 0, 0)),
            scratch_shapes=[pltpu.VMEM((past, 2 * KV_W), F32), pltpu.SemaphoreType.DMA(())]),
        out_shape=jax.ShapeDtypeStruct((nb, N_KV, total, LANE), BF16),
        compiler_params=_cparams("arbitrary"),
        name="gather_sel",
    )(page_table, cache, new_kvg)


ATT_TQ = LANE
ATT_TK = 512
WIN_ROWS = WINDOW + ATT_TQ


def _top_rows(w, idx, n_top):
    alive = idx
    for _ in range(n_top):
        m = jnp.max(w, axis=0, keepdims=True)
        pick = jnp.min(jnp.where(w == m, alive, BIG_I), axis=0, keepdims=True)
        hit = idx == pick
        w = jnp.where(hit, -jnp.inf, w)
        alive = jnp.where(hit, BIG_I, alive)
    return alive


def _attn_kernel(q_ref, ng_ref, kc_ref, kct_ref, cpos_ref, ks_ref, kst_ref, kw_ref, kwt_ref, o_ref,
                 m_s, l_s, acc_s, sel_s, sh_s, gt_s, *, q_base, win_base):
    tq, tk = ATT_TQ, ATT_TK
    g = pl.program_id(1)
    t0 = q_base + pl.program_id(2) * tq
    t0f = t0.astype(F32)
    ncp = kc_ref.shape[2]
    ncp4 = ncp // RATIO
    nsp = sel_s.shape[0]
    slopes = [jnp.where(g == 0, 2.0 ** -(h + 1), 2.0 ** -(GROUP + h + 1)).astype(F32) for h in range(GROUP)]
    hs = [slice(h * tq, (h + 1) * tq) for h in range(GROUP)]
    q = q_ref[0]
    qt = jnp.concatenate([q[:, h * LANE:(h + 1) * LANE].astype(F32).T for h in range(GROUP)], axis=1).astype(BF16)
    lane_t = t0 + lax.broadcasted_iota(jnp.int32, (1, tq), 1)

    sc = _dot(kc_ref[0, 0], qt)
    cpos = cpos_ref[...]
    valid_c = cpos <= lane_t.astype(F32)
    dist = cpos - t0f
    imp = jnp.zeros((ncp, tq), F32)
    pn_parts = []
    for h in range(GROUP):
        s = jnp.where(valid_c, sc[:, hs[h]] + slopes[h] * dist, NEG)
        m = jnp.max(s, axis=0, keepdims=True)
        p = jnp.where(valid_c, jnp.exp(s - m), 0.0)
        pn = p * (1.0 / jnp.maximum(jnp.sum(p, axis=0, keepdims=True), 1e-30))
        imp = imp + pn
        pn_parts.append(pn.astype(BF16))
    oc = _dot(kct_ref[0, 0], jnp.concatenate(pn_parts, axis=1))

    imp3 = imp[3 * ncp4:4 * ncp4]
    sh_s[0:SUBLANE, :] = jnp.zeros((SUBLANE, tq), F32)
    sh_s[SUBLANE:SUBLANE + ncp4, :] = imp3
    pslc = imp[0:ncp4] + imp[ncp4:2 * ncp4] + imp[2 * ncp4:3 * ncp4] + imp3 + sh_s[pl.ds(SUBLANE - 1, ncp4), :]
    if nsp > ncp4:
        pslc = jnp.concatenate([pslc, jnp.zeros((nsp - ncp4, tq), F32)], axis=0)
    blk = lax.broadcasted_iota(jnp.int32, (nsp, tq), 0)
    cur = jnp.right_shift(lane_t, 6)
    forced = (blk == 0) | (blk == cur) | (blk == cur - 1)
    w = jnp.where(forced, jnp.inf, jnp.where(blk <= cur, pslc, -jnp.inf))
    alive = _top_rows(w, blk, SEL_TOP)
    sel_s[...] = jnp.where(alive == BIG_I, jnp.where(blk <= cur, 1.0, 0.0), 0.0)

    m_s[...] = jnp.full(m_s.shape, NEG, F32)
    l_s[...] = jnp.zeros(l_s.shape, F32)
    acc_s[...] = jnp.zeros(acc_s.shape, F32)
    rowf = lax.broadcasted_iota(jnp.int32, (tk, tq), 0).astype(F32)
    dtab = lax.broadcasted_iota(jnp.int32, (tk, tq), 0) - lax.broadcasted_iota(jnp.int32, (tk, tq), 1)

    def key_tile(kt, carry):
        k0 = pl.multiple_of(kt * tk, tk)
        st = _dot(ks_ref[0, 0, pl.ds(k0, tk), :], qt)
        selm = jnp.concatenate(
            [jnp.broadcast_to(sel_s[pl.ds(kt * (tk // SEL_BLOCK) + j, 1), :], (SEL_BLOCK, tq))
             for j in range(tk // SEL_BLOCK)], axis=0)
        valid = jnp.where(dtab <= t0 - k0, selm, 0.0) > 0.5
        rel = rowf + (k0 - t0).astype(F32)
        p_parts, a_parts = [], []
        for h in range(GROUP):
            s = jnp.where(valid, st[:, hs[h]] + slopes[h] * rel, NEG)
            m_old = m_s[:, hs[h]]
            m_new = jnp.maximum(m_old, jnp.max(s, axis=0, keepdims=True))
            a = jnp.exp(m_old - m_new)
            p = jnp.exp(s - m_new)
            l_s[:, hs[h]] = a * l_s[:, hs[h]] + jnp.sum(p, axis=0, keepdims=True)
            m_s[:, hs[h]] = m_new
            p_parts.append(p.astype(BF16))
            a_parts.append(a)
        pv = _dot(kst_ref[0, 0, :, pl.ds(k0, tk)], jnp.concatenate(p_parts, axis=1))
        acc_s[...] = acc_s[...] * jnp.concatenate(a_parts, axis=1) + pv
        return carry

    lax.fori_loop(0, (t0 + tq - 1) // tk + 1, key_tile, 0)
    osel = acc_s[...] * (1.0 / l_s[...])

    kr0 = pl.multiple_of(jnp.maximum(t0 - WINDOW - win_base, 0), LANE)
    sw = _dot(kw_ref[0, 0, pl.ds(kr0, WIN_ROWS), :], qt)
    wrow = lax.broadcasted_iota(jnp.int32, (WIN_ROWS, tq), 0)
    dw = (lane_t - win_base - kr0) - wrow
    valid_w = (dw >= 0) & (dw < WINDOW)
    relw = wrow.astype(F32) + (win_base + kr0 - t0).astype(F32)
    pw_parts = []
    for h in range(GROUP):
        s = jnp.where(valid_w, sw[:, hs[h]] + slopes[h] * relw, NEG)
        m = jnp.max(s, axis=0, keepdims=True)
        p = jnp.where(valid_w, jnp.exp(s - m), 0.0)
        pn = p * (1.0 / jnp.maximum(jnp.sum(p, axis=0, keepdims=True), 1e-30))
        pw_parts.append(pn.astype(BF16))
    ow = _dot(kwt_ref[0, 0, :, pl.ds(kr0, WIN_ROWS)], jnp.concatenate(pw_parts, axis=1))

    gt_s[...] = ng_ref[0].T
    o_parts = []
    for h in range(GROUP):
        grow = 3 * (GROUP * g + h)
        o_parts.append(gt_s[pl.ds(grow, 1), :] * oc[HEAD_DIM:, hs[h]]
                       + gt_s[pl.ds(grow + 1, 1), :] * osel[HEAD_DIM:, hs[h]]
                       + gt_s[pl.ds(grow + 2, 1), :] * ow[HEAD_DIM:, hs[h]])
    o_ref[0] = jnp.concatenate(o_parts, axis=0).T.astype(o_ref.dtype)


def _cmp_positions(n_cmp_rows, n_valid):
    p = np.arange(n_cmp_rows)
    c = RATIO * (p % (n_cmp_rows // RATIO)) + p // (n_cmp_rows // RATIO)
    pos = np.where(c < n_valid, c * CMP_STRIDE + CMP_BLOCK - 1, 1e9).astype(np.float32)
    return jnp.asarray(np.broadcast_to(pos[:, None], (n_cmp_rows, LANE)))


def _attention(q, ng, kc, ks, kw, *, n_cmp_valid, q_base, win_base):
    nb, lq, _ = q.shape
    ncp, tkp, wr = kc.shape[2], ks.shape[2], kw.shape[2]
    nsp = tkp // SEL_BLOCK
    kc = kc.reshape(nb, N_KV, ncp // RATIO, RATIO, LANE).transpose(0, 1, 3, 2, 4).reshape(nb, N_KV, ncp, LANE)
    tr = lambda x: jnp.swapaxes(x, 2, 3)
    cpos = _cmp_positions(ncp, n_cmp_valid)
    r = GROUP * ATT_TQ
    res = lambda rows, cols: pl.BlockSpec((1, 1, rows, cols), lambda b, g, i: (b, g, 0, 0), pipeline_mode=pl.Buffered(1))
    return pl.pallas_call(
        functools.partial(_attn_kernel, q_base=q_base, win_base=win_base),
        grid=(nb, N_KV, lq // ATT_TQ),
        in_specs=[pl.BlockSpec((1, ATT_TQ, GROUP * LANE), lambda b, g, i: (b, i, g)),
                  pl.BlockSpec((1, ATT_TQ, LANE), lambda b, g, i: (b, i, 0)),
                  res(ncp, LANE), res(LANE, ncp),
                  pl.BlockSpec((ncp, LANE), lambda b, g, i: (0, 0), pipeline_mode=pl.Buffered(1)),
                  res(tkp, LANE), res(LANE, tkp), res(wr, LANE), res(LANE, wr)],
        out_specs=pl.BlockSpec((1, ATT_TQ, GROUP * HEAD_DIM), lambda b, g, i: (b, i, g)),
        out_shape=jax.ShapeDtypeStruct((nb, lq, Q_W), BF16),
        scratch_shapes=[pltpu.VMEM((1, r), F32), pltpu.VMEM((1, r), F32), pltpu.VMEM((LANE, r), F32),
                        pltpu.VMEM((nsp, ATT_TQ), F32), pltpu.VMEM((ncp // RATIO + SUBLANE, ATT_TQ), F32),
                        pltpu.VMEM((LANE, ATT_TQ), F32)],
        compiler_params=_cparams("arbitrary", "arbitrary", "arbitrary"),
        name="nsa_attention",
    )(q, ng, kc, tr(kc), cpos, ks, tr(ks), kw, tr(kw))


HIST = 32


def _mix_kernel(x_ref, a_ref, hist_ref, o_ref, ga_ref, gb_ref, g1_ref, dw_ref, dwb_ref, cg_ref, cb_ref,
                wco_ref, wno_ref, wout_ref, l1g_ref, l1b_ref, x1_ref, ext_s, *, zero_first_hist):
    sb, lb, d = x_ref.shape
    tm = sb * lb
    hist = hist_ref[...]
    if zero_first_hist:
        hist = jnp.where(pl.program_id(1) > 0, hist, 0.0)
    ext_s[:, 0:HIST, :] = hist
    ext_s[:, HIST:HIST + lb, :] = a_ref[...]
    y = jnp.zeros((sb, lb, D_CONV), F32) + dwb_ref[...]
    for j in range(CONV_W):
        y = y + dw_ref[j:j + 1, :] * ext_s[:, pl.ds(j + HIST - (CONV_W - 1), lb), :]
    yn = _ln(y) * cg_ref[...] + cb_ref[...]
    act = (yn * _sigmoid(yn)).reshape(tm, D_CONV)
    out_a = _dot(act.astype(BF16), wco_ref[...])
    out_b = _dot(o_ref[...].reshape(tm, Q_W), wno_ref[...])
    merged = (ga_ref[...].reshape(tm, d).astype(F32) * out_a + gb_ref[...].reshape(tm, d).astype(F32) * out_b)
    mix = _dot(merged.astype(BF16), wout_ref[...]).reshape(sb, lb, d)
    x1_ref[...] = _ln(ALPHA * x_ref[...] + g1_ref[...] * mix) * l1g_ref[...] + l1b_ref[...]


def _mix(x, a, hist, o, ga, gb, mod, lw, tm, *, hist_from_a):
    nseq, seqlen, d = x.shape
    sb, lb, grid = _token_blocks(nseq, seqlen, tm)
    tok = lambda width: pl.BlockSpec((sb, lb, width), lambda s, t: (s, t, 0))
    if hist_from_a:
        hist_spec = pl.BlockSpec((sb, HIST, D_CONV), lambda s, t: (s, jnp.maximum(t * (lb // HIST) - 1, 0), 0))
    else:
        hist_spec = pl.BlockSpec((sb, HIST, D_CONV), lambda s, t: (s, 0, 0))
    const = lambda arr: pl.BlockSpec(arr.shape, lambda s, t: (0,) * arr.ndim, pipeline_mode=pl.Buffered(1))
    return pl.pallas_call(
        functools.partial(_mix_kernel, zero_first_hist=hist_from_a),
        grid=grid,
        in_specs=[tok(d), tok(D_CONV), hist_spec, tok(Q_W), tok(d), tok(d),
                  pl.BlockSpec((sb, 1, d), lambda s, t: (s, 0, 2))] + [const(w) for w in lw],
        out_specs=tok(d),
        out_shape=jax.ShapeDtypeStruct((nseq, seqlen, d), F32),
        scratch_shapes=[pltpu.VMEM((sb, HIST + lb, D_CONV), F32)],
        compiler_params=_cparams("parallel", "arbitrary"),
        name="mix",
    )(x, a, hist, o, ga, gb, mod, *lw)


NO_RANK = 99
CAND_ROWS = PEER_TOPK + SUBLANE * (PEER_TOPK - 1)


def _top_ranked(w, idx, n_top):
    rank = jnp.full(w.shape, NO_RANK, jnp.int32)
    vals = []
    for k in range(n_top):
        m = jnp.max(w, axis=0, keepdims=True)
        hit = idx == jnp.min(jnp.where(w == m, idx, BIG_I), axis=0, keepdims=True)
        w = jnp.where(hit, -jnp.inf, w)
        rank = jnp.where(hit, k, rank)
        vals.append(m)
    return rank, vals


def _peer_sel_kernel(x1_ref, sh_ref, sc_ref, wq_ref, keys_ref, h2t_ref, n0_ref, e0_ref, r1_ref, e1_ref, q_s):
    sb, lb, d = x1_ref.shape
    tm = sb * lb
    h2 = (_ln(x1_ref[...]) * (1.0 + sc_ref[...]) + sh_ref[...]).reshape(tm, d)
    h2t_ref[...] = h2.T.astype(BF16)
    q_s[...] = _dot(h2.astype(BF16), wq_ref[...]).astype(BF16)
    idx = lax.broadcasted_iota(jnp.int32, (N_KEYS, tm), 0)
    row8 = lax.broadcasted_iota(jnp.int32, (SUBLANE, tm), 0)
    cidx = lax.broadcasted_iota(jnp.int32, (CAND_ROWS, tm), 0)
    nt = (((1,), (1,)), ((), ()))

    def head(h, carry):
        s0 = lax.dot_general(keys_ref[2 * h], q_s[:, pl.ds(pl.multiple_of(2 * h * N_KEYS, N_KEYS), N_KEYS)], nt,
                             preferred_element_type=F32)
        s1 = lax.dot_general(keys_ref[2 * h + 1], q_s[:, pl.ds(pl.multiple_of((2 * h + 1) * N_KEYS, N_KEYS), N_KEYS)],
                             nt, preferred_element_type=F32)
        rank0, v0 = _top_ranked(s0, idx, PEER_TOPK)
        rank1, v1 = _top_ranked(s1, idx, PEER_TOPK)
        v1a = jnp.concatenate(v1[:SUBLANE], axis=0)
        v1b = jnp.concatenate(v1[SUBLANE:], axis=0)
        pieces = [v1a + v0[0], v1b + v0[0]]
        for i in range(1, PEER_TOPK):
            pieces.append(jnp.where(row8 < PEER_TOPK // (i + 1), v1a + v0[i], -jnp.inf))
        cand = jnp.concatenate(pieces, axis=0)
        crank, _ = _top_ranked(cand, cidx, PEER_TOPK)
        chosen = crank < PEER_TOPK
        top = v0[0] + v1[0]
        z = jnp.sum(jnp.where(chosen, jnp.exp(cand - top), 0.0), axis=0, keepdims=True)
        cf = jnp.where(chosen, 1.0, 0.0)
        counts = [jnp.sum(cf[0:2 * SUBLANE], axis=0, keepdims=True)]
        for i in range(1, PEER_TOPK):
            counts.append(jnp.sum(cf[SUBLANE * (i + 1):SUBLANE * (i + 2)], axis=0, keepdims=True))
        n0 = jnp.zeros((N_KEYS, tm), F32)
        for i in range(PEER_TOPK):
            n0 = jnp.where(rank0 == i, counts[i], n0)
        n0_ref[h] = n0
        e0_ref[h] = jnp.where(rank0 < PEER_TOPK, jnp.exp(s0 - v0[0]), 0.0)
        r1_ref[h] = rank1.astype(F32)
        e1_ref[h] = jnp.where(rank1 < PEER_TOPK, jnp.exp(s1 - v1[0]), 0.0) * (1.0 / z)
        return carry

    lax.fori_loop(0, PEER_HEADS, head, 0)


def _peer_sel(x1, mod, wq, keys, tm):
    nseq, seqlen, d = x1.shape
    n = nseq * seqlen
    sb, lb, grid = _token_blocks(nseq, seqlen, tm)
    tm = sb * lb
    tok = pl.BlockSpec((sb, lb, d), lambda s, t: (s, t, 0))
    flat = lambda s, t: s * grid[1] + t
    stat = pl.BlockSpec((PEER_HEADS, N_KEYS, tm), lambda s, t: (0, 0, flat(s, t)))
    const = lambda arr: pl.BlockSpec(arr.shape, lambda s, t: (0,) * arr.ndim, pipeline_mode=pl.Buffered(1))
    stat_shape = jax.ShapeDtypeStruct((PEER_HEADS, N_KEYS, n), F32)
    return pl.pallas_call(
        _peer_sel_kernel,
        grid=grid,
        in_specs=[tok, pl.BlockSpec((sb, 1, d), lambda s, t: (s, 0, 3)), pl.BlockSpec((sb, 1, d), lambda s, t: (s, 0, 4)),
                  const(wq), const(keys)],
        out_specs=[pl.BlockSpec((d, tm), lambda s, t: (0, flat(s, t))), stat, stat, stat, stat],
        out_shape=[jax.ShapeDtypeStruct((d, n), BF16), stat_shape, stat_shape, stat_shape, stat_shape],
        scratch_shapes=[pltpu.VMEM((tm, 2 * PEER_HEADS * N_KEYS), BF16)],
        compiler_params=_cparams("parallel", "parallel"),
        name="peer_select",
    )(x1, mod, mod, wq, keys)


PEER_CE = 1024


def _peer_ffn_kernel(h2t_ref, u_ref, vt_ref, n0_ref, e0_ref, r1_ref, e1_ref, x1_ref, g2_ref, l2g_ref, l2b_ref,
                     y_ref, acc_s, wg_s):
    c = pl.program_id(2)

    @pl.when(c == 0)
    def _():
        acc_s[...] = jnp.zeros(acc_s.shape, F32)

    ut = _dot(u_ref[...], h2t_ref[...])
    for e in range(PEER_CE // N_KEYS):
        rows = slice(e * N_KEYS, (e + 1) * N_KEYS)
        w = jnp.zeros((N_KEYS, ut.shape[1]), F32)
        for h in range(PEER_HEADS):
            w = w + jnp.where(r1_ref[h] < n0_ref[h, e:e + 1, :], e1_ref[h], 0.0) * e0_ref[h, e:e + 1, :]
        wg_s[rows, :] = (w * _gelu(ut[rows])).astype(BF16)
    acc_s[...] += _dot(vt_ref[...], wg_s[...])

    @pl.when(c == pl.num_programs(2) - 1)
    def _():
        sb, lb, d = x1_ref.shape
        f = acc_s[...].T.reshape(sb, lb, d)
        y_ref[...] = _ln(ALPHA * x1_ref[...] + g2_ref[...] * f) * l2g_ref[...] + l2b_ref[...]


def _peer_ffn(h2t, u, vt, stats, x1, mod, l2g, l2b, tm):
    nseq, seqlen, d = x1.shape
    sb, lb, grid = _token_blocks(nseq, seqlen, tm)
    tm = sb * lb
    n_exp = u.shape[0]
    flat = lambda s, t: s * grid[1] + t
    tok = pl.BlockSpec((sb, lb, d), lambda s, t, c: (s, t, 0))
    c1 = PEER_CE // N_KEYS
    row_stat = pl.BlockSpec((PEER_HEADS, c1, tm), lambda s, t, c: (0, c, flat(s, t)))
    slab_stat = pl.BlockSpec((PEER_HEADS, N_KEYS, tm), lambda s, t, c: (0, 0, flat(s, t)))
    vec = pl.BlockSpec((1, d), lambda s, t, c: (0, 0))
    n0, e0, r1, e1 = stats
    return pl.pallas_call(
        _peer_ffn_kernel,
        grid=grid + (n_exp // PEER_CE,),
        in_specs=[pl.BlockSpec((d, tm), lambda s, t, c: (0, flat(s, t))),
                  pl.BlockSpec((PEER_CE, d), lambda s, t, c: (c, 0)),
                  pl.BlockSpec((d, PEER_CE), lambda s, t, c: (0, c)),
                  row_stat, row_stat, slab_stat, slab_stat, tok,
                  pl.BlockSpec((sb, 1, d), lambda s, t, c: (s, 0, 5)), vec, vec],
        out_specs=tok,
        out_shape=jax.ShapeDtypeStruct((nseq, seqlen, d), F32),
        scratch_shapes=[pltpu.VMEM((d, tm), F32), pltpu.VMEM((PEER_CE, tm), BF16)],
        compiler_params=_cparams("parallel", "parallel", "arbitrary"),
        name="peer_experts",
    )(h2t, u, vt, n0, e0, r1, e1, x1, mod, l2g, l2b)


TOKEN_TILE = 512
PEER_SEL_TILE = 256


def _heads_major(x):
    b, t, _ = x.shape
    return x.reshape(b, t, N_KV, LANE).transpose(0, 2, 1, 3)


def _token_mixing(x, mod, a, q, ng, kc, ks, kw, ga, gb, hist, lw_mix, peer, *, n_cmp_valid, q_base, win_base,
                  hist_from_a):
    lq = x.shape[1]
    pad = (-lq) % ATT_TQ
    if pad:
        q = jnp.pad(q, ((0, 0), (0, pad), (0, 0)))
        ng = jnp.pad(ng, ((0, 0), (0, pad), (0, 0)))
    o = _attention(q, ng, kc, ks, kw, n_cmp_valid=n_cmp_valid, q_base=q_base, win_base=win_base)[:, :lq]
    x1 = _mix(x, a, hist, o, ga, gb, mod, lw_mix, TOKEN_TILE, hist_from_a=hist_from_a)
    wq, keys, u, vt, l2g, l2b = peer
    h2t, *stats = _peer_sel(x1, mod, wq, keys, PEER_SEL_TILE)
    return _peer_ffn(h2t, u, vt, stats, x1, mod, l2g, l2b, TOKEN_TILE)


def kernel(x_prompt, x_sample, cache_cmp_kv, cache_sel_kv, state_win_kv, state_conv, page_table, c_prompt, c_sample,
           w_ada, b_ada, w_in, b_in, conv_dw, conv_dw_b, conv_ln_g, conv_ln_b, w_conv_out, cmp_pos, w_cmp_k1, w_cmp_k2,
           w_cmp_v1, w_cmp_v2, w_nsa_out, w_out, ln1_g, ln1_b, peer_wq, peer_keys, peer_u, peer_v, ln2_g, ln2_b):
    assert w_ada.shape[0] == DEPTH == 1
    l = 0
    nbp, s, d = x_prompt.shape
    nbs, ls, _ = x_sample.shape
    n_pages = page_table.shape[1]
    past = n_pages * PAGE_SIZE
    nbuf = state_win_kv.shape[2]
    n_phys = cache_cmp_kv.shape[1]
    assert s % (CMP_ROWS * CMP_STRIDE) == 0 and past % (CMP_ROWS * CMP_STRIDE) == 0 and nbuf == WINDOW

    w_ext, b_ext = _extend_w_in(w_in[l], b_in[l])
    cw = _compress_weights(cmp_pos[l], w_cmp_k1[l], w_cmp_k2[l], w_cmp_v1[l], w_cmp_v2[l])
    row = lambda v: v.reshape(1, -1).astype(F32)
    lw_mix = (jnp.pad(conv_dw[l], ((0, HIST - CONV_W), (0, 0))), row(conv_dw_b[l]), row(conv_ln_g[l]), row(conv_ln_b[l]),
              w_conv_out[l].astype(BF16), w_nsa_out[l].astype(BF16), w_out[l].astype(BF16), row(ln1_g[l]), row(ln1_b[l]))
    peer = (peer_wq[l].astype(BF16), peer_keys[l].reshape(2 * PEER_HEADS, N_KEYS, -1).astype(BF16),
            peer_u[l].astype(BF16), peer_v[l].T.astype(BF16), row(ln2_g[l]), row(ln2_b[l]))

    c_all = jnp.concatenate([c_prompt, c_sample], axis=0)
    c_all = jnp.pad(c_all, ((0, (-c_all.shape[0]) % SUBLANE), (0, 0)))
    mod = _adaln(c_all, w_ada[l], b_ada[l])
    mod_p = mod[:nbp].reshape(nbp, 1, -1)
    mod_s = mod[nbp:nbp + nbs].reshape(nbs, 1, -1)

    a_p, q_p, kv_p, kvg_p, ng_p, ga_p, gb_p = _inproj(x_prompt, mod_p, w_ext, b_ext, TOKEN_TILE)
    kc_p = _compress_prompt(kv_p[:, :, :2 * KV_W].reshape(nbp, 2 * s, LANE), cw)
    y_p = _token_mixing(
        x_prompt, mod_p, a_p, q_p, ng_p, _heads_major(kc_p),
        _heads_major(kvg_p[:, :, 2 * KV_W:4 * KV_W]), _heads_major(kvg_p[:, :, 4 * KV_W:]),
        ga_p, gb_p, a_p, lw_mix, peer, n_cmp_valid=s // CMP_STRIDE - 1, q_base=0, win_base=0, hist_from_a=True)

    a_s, q_s, kv_s, kvg_s, ng_s, ga_s, gb_s = _inproj(x_sample, mod_s, w_ext, b_ext, TOKEN_TILE)
    kc_s = _compress_sample(cache_cmp_kv[l].reshape(n_phys, 2 * PAGE_SIZE, LANE), page_table, cw)
    tkp = ((past + ATT_TQ - 1) // ATT_TK + 1) * ATT_TK
    ks_s = _gather_sel(cache_sel_kv[l].reshape(n_phys, PAGE_SIZE, 2 * KV_W), page_table,
                       kvg_s[:, :, 2 * KV_W:4 * KV_W], tkp)
    win_old = state_win_kv[l].transpose(0, 3, 1, 2, 4).reshape(nbs, N_KV, nbuf, LANE).astype(BF16)
    kw_s = jnp.concatenate([win_old, _heads_major(kvg_s[:, :, 4 * KV_W:]),
                            jnp.zeros((nbs, N_KV, WIN_ROWS - nbuf - ls, LANE), BF16)], axis=2)
    hist_s = jnp.pad(state_conv[l], ((0, 0), (HIST - (CONV_W - 1), 0), (0, 0)))
    y_s = _token_mixing(
        x_sample, mod_s, a_s, q_s, ng_s, _heads_major(kc_s), ks_s, kw_s, ga_s, gb_s, hist_s, lw_mix, peer,
        n_cmp_valid=past // CMP_STRIDE - 1, q_base=past, win_base=past - nbuf, hist_from_a=False)

    kv6 = lambda t, c: t[:, :, c * 2 * KV_W:(c + 1) * 2 * KV_W].reshape(1, t.shape[0], t.shape[1], 2, N_KV, HEAD_DIM)
    nb = min(WINDOW, s)
    win_new = jnp.concatenate([state_win_kv[l], kv6(kv_s, 2)[0]], axis=1)[:, ls:]
    conv_new = jnp.concatenate([state_conv[l], a_s], axis=1)[:, ls:]
    return (y_p, y_s, kv6(kv_p, 0), kv6(kv_p, 1), kv6(kv_p, 2)[:, :, s - nb:], a_p[None, :, s - (CONV_W - 1):],
            kv6(kv_s, 0), kv6(kv_s, 1), win_new[None], conv_new[None])
```

```python
import functools

import numpy as np
import jax
import jax.numpy as jnp
from jax import lax
from jax.experimental import pallas as pl
from jax.experimental.pallas import tpu as pltpu

F32 = jnp.float32
BF16 = jnp.bfloat16

D_MODEL = 1024
D_CONV = 512
CONV_W = 31
N_HEADS = 8
HEAD_DIM = 64
N_KV = 2
GROUP = N_HEADS // N_KV
CMP_BLOCK = 32
CMP_STRIDE = 16
CMP_HID = 256
SEL_BLOCK = 64
SEL_TOP = 16
RATIO = SEL_BLOCK // CMP_STRIDE
WINDOW = 512
PAGE_SIZE = 128
PEER_HEADS = 8
N_KEYS = 128
PEER_TOPK = 16
DEPTH = 1
ALPHA = (2.0 * DEPTH) ** 0.25
LN_EPS = 1e-5
Q_W = N_HEADS * HEAD_DIM
KV_W = N_KV * HEAD_DIM

LANE = 128
SUBLANE = 8
VMEM_LIMIT = 56 * 1024 * 1024
NEG = -1e30
BIG_I = 1 << 20

SEG_A = 2 * D_CONV
SEG_Q = N_HEADS * LANE
SEG_KV = 6 * KV_W
SEG_KVG = 6 * KV_W
SEG_G = LANE
SEG_M = 2 * D_MODEL
OFF_A = 0
OFF_Q = OFF_A + SEG_A
OFF_KV = OFF_Q + SEG_Q
OFF_KVG = OFF_KV + SEG_KV
OFF_G = OFF_KVG + SEG_KVG
OFF_M = OFF_G + SEG_G
W_EXT = OFF_M + SEG_M

TN_DIMS = (((0,), (0,)), ((), ()))
NT_DIMS = (((1,), (1,)), ((), ()))


def _cparams(*sem):
    return pltpu.CompilerParams(dimension_semantics=sem, vmem_limit_bytes=VMEM_LIMIT)


def _ln(x):
    mu = jnp.mean(x, axis=-1, keepdims=True)
    xc = x - mu
    return xc * lax.rsqrt(jnp.mean(xc * xc, axis=-1, keepdims=True) + LN_EPS)


def _sigmoid(x):
    return 1.0 / (1.0 + jnp.exp(-x))


def _gelu(x):
    return 0.5 * x * (1.0 + jnp.tanh(0.7978845608028654 * (x + 0.044715 * (x * x * x))))


def _dot(a, b):
    return jnp.dot(a, b, preferred_element_type=F32)


def _adaln_kernel(c_ref, w_ref, b_ref, o_ref):
    c = c_ref[...]
    s = c * _sigmoid(c)
    o_ref[...] = _dot(s.astype(BF16), w_ref[...].astype(BF16)) + b_ref[...]


def _adaln(c, w, b):
    rows, d = c.shape
    n = w.shape[1]
    tn = 1024
    return pl.pallas_call(
        _adaln_kernel,
        grid=(n // tn,),
        in_specs=[pl.BlockSpec((rows, d), lambda j: (0, 0)),
                  pl.BlockSpec((d, tn), lambda j: (0, j)),
                  pl.BlockSpec((1, tn), lambda j: (0, j))],
        out_specs=pl.BlockSpec((rows, tn), lambda j: (0, j)),
        out_shape=jax.ShapeDtypeStruct((rows, n), F32),
        compiler_params=_cparams("parallel"),
        name="adaln",
    )(c, w, b.reshape(1, n))


def _inproj_kernel(x_ref, mod_ref, w_ref, b_ref, a_ref, q_ref, kv_ref, kvg_ref, ng_ref, ga_ref, gb_ref):
    sb, lb, d = x_ref.shape
    tm = sb * lb
    mod = mod_ref[...]
    h = _ln(x_ref[...]) * (1.0 + mod[:, :, d:2 * d]) + mod[:, :, 0:d]
    hb = h.reshape(tm, d).astype(BF16)

    def seg(off, width):
        return _dot(hb, w_ref[:, off:off + width]) + b_ref[:, off:off + width]

    z = seg(OFF_A, SEG_A)
    a_ref[...] = (z[:, :D_CONV] * _sigmoid(z[:, D_CONV:])).reshape(sb, lb, D_CONV)
    q_ref[...] = seg(OFF_Q, SEG_Q).astype(BF16).reshape(sb, lb, SEG_Q)
    kv_ref[...] = seg(OFF_KV, SEG_KV).reshape(sb, lb, SEG_KV)
    kvg_ref[...] = seg(OFF_KVG, SEG_KVG).astype(BF16).reshape(sb, lb, SEG_KVG)
    ng_ref[...] = _sigmoid(seg(OFF_G, SEG_G)).reshape(sb, lb, SEG_G)
    ga_ref[...] = _sigmoid(seg(OFF_M, D_MODEL)).astype(BF16).reshape(sb, lb, D_MODEL)
    gb_ref[...] = _sigmoid(seg(OFF_M + D_MODEL, D_MODEL)).astype(BF16).reshape(sb, lb, D_MODEL)


def _token_blocks(nseq, seqlen, tm):
    if seqlen >= tm:
        return 1, tm, (nseq, seqlen // tm)
    sb = min(tm // seqlen, nseq)
    return sb, seqlen, (nseq // sb, 1)


def _inproj(x, mod, w_ext, b_ext, tm):
    nseq, seqlen, d = x.shape
    sb, lb, grid = _token_blocks(nseq, seqlen, tm)
    tok = lambda width: pl.BlockSpec((sb, lb, width), lambda s, t: (s, t, 0))
    widths = (D_CONV, SEG_Q, SEG_KV, SEG_KVG, SEG_G, D_MODEL, D_MODEL)
    dtypes = (F32, BF16, F32, BF16, F32, BF16, BF16)
    return pl.pallas_call(
        _inproj_kernel,
        grid=grid,
        in_specs=[tok(d),
                  pl.BlockSpec((sb, 1, 2 * d), lambda s, t: (s, 0, 0)),
                  pl.BlockSpec((d, W_EXT), lambda s, t: (0, 0), pipeline_mode=pl.Buffered(1)),
                  pl.BlockSpec((1, W_EXT), lambda s, t: (0, 0), pipeline_mode=pl.Buffered(1))],
        out_specs=[tok(w) for w in widths],
        out_shape=[jax.ShapeDtypeStruct((nseq, seqlen, w), dt) for w, dt in zip(widths, dtypes)],
        compiler_params=_cparams("parallel", "parallel"),
        name="inproj",
    )(x, mod, w_ext, b_ext)


def _extend_w_in(w_in, b_in):
    scale = HEAD_DIM ** -0.5
    c_q = 2 * D_CONV
    c_kv = c_q + Q_W
    c_g = c_kv + 6 * KV_W
    c_m = c_g + 3 * N_HEADS
    cols_w, cols_b = [], []

    def add(w, b):
        cols_w.append(w)
        cols_b.append(b)

    add(w_in[:, :c_q], b_in[:c_q])
    zw = jnp.zeros((w_in.shape[0], HEAD_DIM), w_in.dtype)
    zb = jnp.zeros((HEAD_DIM,), b_in.dtype)
    for h in range(N_HEADS):
        sl = slice(c_q + h * HEAD_DIM, c_q + (h + 1) * HEAD_DIM)
        add(w_in[:, sl] * scale, b_in[sl] * scale)
        add(zw, zb)
    add(w_in[:, c_kv:c_g], b_in[c_kv:c_g])
    for br in range(3):
        k0 = c_kv + br * 2 * KV_W
        v0 = k0 + KV_W
        for g in range(N_KV):
            ks = slice(k0 + g * HEAD_DIM, k0 + (g + 1) * HEAD_DIM)
            vs = slice(v0 + g * HEAD_DIM, v0 + (g + 1) * HEAD_DIM)
            add(w_in[:, ks], b_in[ks])
            add(w_in[:, vs], b_in[vs])
    add(w_in[:, c_g:c_m], b_in[c_g:c_m])
    add(jnp.zeros((w_in.shape[0], SEG_G - 3 * N_HEADS), w_in.dtype), jnp.zeros((SEG_G - 3 * N_HEADS,), b_in.dtype))
    add(w_in[:, c_m:], b_in[c_m:])
    w = jnp.concatenate(cols_w, axis=1).astype(BF16)
    b = jnp.concatenate(cols_b, axis=0).reshape(1, -1).astype(F32)
    assert w.shape[1] == W_EXT
    return w, b


CMP_ROWS = 128


def _compress_rows(load_k, load_v, pos_ref, w1k_ref, w1v_ref, w2_ref):
    acc_k = jnp.zeros((CMP_ROWS, 2 * CMP_HID), F32)
    acc_v = jnp.zeros((CMP_ROWS, 2 * CMP_HID), F32)
    for j in range(0, CMP_BLOCK, 2):
        pos = [pos_ref[j + i:j + i + 1, :] for i in range(2)]
        xk = jnp.concatenate([(load_k(j + i) + pos[i]).astype(BF16) for i in range(2)], axis=1)
        xv = jnp.concatenate([(load_v(j + i) + pos[i]).astype(BF16) for i in range(2)], axis=1)
        acc_k = acc_k + _dot(xk, w1k_ref[j // 2])
        acc_v = acc_v + _dot(xv, w1v_ref[j // 2])
    hk = _gelu(acc_k)
    hv = _gelu(acc_v)
    outs = []
    for g in range(N_KV):
        hg = jnp.concatenate([hk[:, g * CMP_HID:(g + 1) * CMP_HID], hv[:, g * CMP_HID:(g + 1) * CMP_HID]], axis=1)
        outs.append(_dot(hg.astype(BF16), w2_ref[...]))
    return jnp.concatenate(outs, axis=1)


def _compress_weights(cmp_pos, w_k1, w_k2, w_v1, w_v2):
    def bd1(w1):
        w = w1.reshape(CMP_BLOCK, HEAD_DIM, CMP_HID)
        z = jnp.zeros_like(w)
        top = jnp.concatenate([w, z], axis=2)
        bot = jnp.concatenate([z, w], axis=2)
        wbd = jnp.concatenate([top, bot], axis=1)
        return wbd.reshape(CMP_BLOCK // 2, 2 * KV_W, 2 * CMP_HID).astype(BF16)
    zk = jnp.zeros_like(w_k2)
    w2 = jnp.concatenate([jnp.concatenate([w_k2, zk], axis=1),
                          jnp.concatenate([zk, w_v2], axis=1)], axis=0).astype(BF16)
    pos2 = jnp.tile(cmp_pos, (1, N_KV)).astype(F32)
    return pos2, bd1(w_k1), bd1(w_v1), w2


def _compress_prompt_kernel(x_ref, halo_ref, pos_ref, w1k_ref, w1v_ref, w2_ref, o_ref, xs_ref, *, n_valid):
    rows2 = x_ref.shape[1]
    xs_ref[0:rows2, :] = x_ref[0]
    xs_ref[rows2:rows2 + 2 * CMP_STRIDE, :] = halo_ref[0]
    n_rows = rows2 // (2 * CMP_STRIDE)
    first = pl.program_id(1) * n_rows
    for r in range(n_rows // CMP_ROWS):
        base = 2 * CMP_STRIDE * CMP_ROWS * r
        load = lambda off: (lambda j: xs_ref[pl.ds(base + 2 * j + off, CMP_ROWS, stride=2 * CMP_STRIDE), :])
        out = _compress_rows(load(0), load(1), pos_ref, w1k_ref, w1v_ref, w2_ref)
        idx = first + r * CMP_ROWS + lax.broadcasted_iota(jnp.int32, out.shape, 0)
        o_ref[0, r * CMP_ROWS:(r + 1) * CMP_ROWS, :] = jnp.where(idx < n_valid, out, 0.0).astype(o_ref.dtype)


def _compress_prompt(kvc, cw):
    nb, s2, _ = kvc.shape
    s = s2 // 2
    pos2, w1k, w1v, w2 = cw
    p = min(2048, s)
    n_ch = s // CMP_STRIDE
    rows = p // CMP_STRIDE
    const = lambda shape: pl.BlockSpec(shape, lambda b, i: (0,) * len(shape), pipeline_mode=pl.Buffered(1))
    return pl.pallas_call(
        functools.partial(_compress_prompt_kernel, n_valid=n_ch - 1),
        grid=(nb, s // p),
        in_specs=[pl.BlockSpec((1, 2 * p, LANE), lambda b, i: (b, i, 0)),
                  pl.BlockSpec((1, 2 * CMP_STRIDE, LANE), lambda b, i: (b, jnp.minimum((i + 1) * rows, n_ch - 1), 0)),
                  const(pos2.shape), const(w1k.shape), const(w1v.shape), const(w2.shape)],
        out_specs=pl.BlockSpec((1, rows, 2 * KV_W), lambda b, i: (b, i, 0)),
        out_shape=jax.ShapeDtypeStruct((nb, n_ch, 2 * KV_W), BF16),
        scratch_shapes=[pltpu.VMEM((2 * p + 2 * CMP_STRIDE, LANE), F32)],
        compiler_params=_cparams("parallel", "parallel"),
        name="compress_prompt",
    )(kvc, kvc, pos2, w1k, w1v, w2)


PAGE_ROWS = 2 * KV_W
PAGE_UNROLL = 4


def _page_copy(cache_ref, pt_ref, buf_ref, sem_ref, b, p, slot):
    return pltpu.make_async_copy(cache_ref.at[pt_ref[b, p]], buf_ref.at[slot, p], sem_ref.at[slot])


def _pages_start(cache_ref, pt_ref, buf_ref, sem_ref, b, slot, n_pages):
    def body(p, c):
        _page_copy(cache_ref, pt_ref, buf_ref, sem_ref, b, p, slot).start()
        return c
    lax.fori_loop(0, n_pages, body, 0)


def _pages_wait(cache_ref, pt_ref, buf_ref, sem_ref, b, slot, n_pages):
    def body(p, c):
        _page_copy(cache_ref, pt_ref, buf_ref, sem_ref, b, p, slot).wait()
        return c
    lax.fori_loop(0, n_pages, body, 0)


def _pages_pipeline(cache_ref, pt_ref, buf_ref, sem_ref, n_pages):
    b = pl.program_id(0)
    slot = lax.rem(b, 2)

    @pl.when(b == 0)
    def _():
        _pages_start(cache_ref, pt_ref, buf_ref, sem_ref, b, slot, n_pages)

    @pl.when(b + 1 < pl.num_programs(0))
    def _():
        _pages_start(cache_ref, pt_ref, buf_ref, sem_ref, b + 1, 1 - slot, n_pages)

    _pages_wait(cache_ref, pt_ref, buf_ref, sem_ref, b, slot, n_pages)
    return slot


def _compress_sample_kernel(pt_ref, cache_ref, pos_ref, w1k_ref, w1v_ref, w2_ref, o_ref, pbuf, xk_s, xv_s, sem,
                            *, n_pages):
    past = n_pages * PAGE_SIZE
    slot = _pages_pipeline(cache_ref, pt_ref, pbuf, sem, n_pages)
    tail = jnp.zeros((CMP_STRIDE, LANE), F32)
    xk_s[past:past + CMP_STRIDE, :] = tail
    xv_s[past:past + CMP_STRIDE, :] = tail

    def to_rows(i, c):
        for j in range(PAGE_UNROLL):
            p = i * PAGE_UNROLL + j
            t = pbuf[slot, p].T
            r0 = pl.multiple_of(p * PAGE_SIZE, PAGE_SIZE)
            xk_s[pl.ds(r0, PAGE_SIZE), :] = t[:, :KV_W]
            xv_s[pl.ds(r0, PAGE_SIZE), :] = t[:, KV_W:]
        return c
    lax.fori_loop(0, n_pages // PAGE_UNROLL, to_rows, 0)
    n_rows = past // CMP_STRIDE

    def step(r, c):
        row0 = pl.multiple_of(r * CMP_ROWS, CMP_ROWS)
        base = row0 * CMP_STRIDE
        out = _compress_rows(lambda j: xk_s[pl.ds(base + j, CMP_ROWS, stride=CMP_STRIDE), :],
                             lambda j: xv_s[pl.ds(base + j, CMP_ROWS, stride=CMP_STRIDE), :],
                             pos_ref, w1k_ref, w1v_ref, w2_ref)
        idx = row0 + lax.broadcasted_iota(jnp.int32, out.shape, 0)
        o_ref[0, pl.ds(row0, CMP_ROWS), :] = jnp.where(idx < n_rows - 1, out, 0.0).astype(o_ref.dtype)
        return c
    lax.fori_loop(0, n_rows // CMP_ROWS, step, 0)


def _compress_sample(cache, page_table, cw):
    pos2, w1k, w1v, w2 = cw
    nb, n_pages = page_table.shape
    past = n_pages * PAGE_SIZE
    const = lambda shape: pl.BlockSpec(shape, lambda b, pt: (0,) * len(shape), pipeline_mode=pl.Buffered(1))
    return pl.pallas_call(
        functools.partial(_compress_sample_kernel, n_pages=n_pages),
        grid_spec=pltpu.PrefetchScalarGridSpec(
            num_scalar_prefetch=1,
            grid=(nb,),
            in_specs=[pl.BlockSpec(memory_space=pl.ANY),
                      const(pos2.shape), const(w1k.shape), const(w1v.shape), const(w2.shape)],
            out_specs=pl.BlockSpec((1, past // CMP_STRIDE, 2 * KV_W), lambda b, pt: (b, 0, 0)),
            scratch_shapes=[pltpu.VMEM((2, n_pages, PAGE_ROWS, PAGE_SIZE), F32),
                            pltpu.VMEM((past + CMP_STRIDE, LANE), F32), pltpu.VMEM((past + CMP_STRIDE, LANE), F32),
                            pltpu.SemaphoreType.DMA((2,))]),
        out_shape=jax.ShapeDtypeStruct((nb, past // CMP_STRIDE, 2 * KV_W), BF16),
        compiler_params=_cparams("arbitrary"),
        name="compress_sample",
    )(page_table, cache, pos2, w1k, w1v, w2)


ATT_TQ = LANE
ATT_TK = 512
WIN_ROWS = WINDOW + ATT_TQ


def _top_rows(w, idx, n_top):
    alive = idx
    for _ in range(n_top):
        m = jnp.max(w, axis=0, keepdims=True)
        pick = jnp.min(jnp.where(w == m, alive, BIG_I), axis=0, keepdims=True)
        hit = idx == pick
        w = jnp.where(hit, -jnp.inf, w)
        alive = jnp.where(hit, BIG_I, alive)
    return alive


def _select_blocks(imp, sh_s, sel_s, lane_t):
    ncp4 = imp.shape[0] // RATIO
    nsp, nq = sel_s.shape
    imp3 = imp[3 * ncp4:4 * ncp4]
    sh_s[0:SUBLANE, :] = jnp.zeros((SUBLANE, nq), F32)
    sh_s[SUBLANE:SUBLANE + ncp4, :] = imp3
    pslc = imp[0:ncp4] + imp[ncp4:2 * ncp4] + imp[2 * ncp4:3 * ncp4] + imp3 + sh_s[pl.ds(SUBLANE - 1, ncp4), :]
    if nsp > ncp4:
        pslc = jnp.concatenate([pslc, jnp.zeros((nsp - ncp4, nq), F32)], axis=0)
    blk = lax.broadcasted_iota(jnp.int32, (nsp, nq), 0)
    cur = jnp.right_shift(lane_t, 6)
    forced = (blk == 0) | (blk == cur) | (blk == cur - 1)
    w = jnp.where(forced, jnp.inf, jnp.where(blk <= cur, pslc, -jnp.inf))
    alive = _top_rows(w, blk, SEL_TOP)
    sel_s[...] = jnp.where(alive == BIG_I, jnp.where(blk <= cur, 1.0, 0.0), 0.0)


def _attn_kernel(q_ref, ng_ref, kc_ref, kct_ref, cpos_ref, ks_ref, kst_ref, kw_ref, kwt_ref, o_ref,
                 m_s, l_s, acc_s, sel_s, sh_s, gt_s, bias_s, *, q_base, win_base):
    tq, tk = ATT_TQ, ATT_TK
    g = pl.program_id(1)
    t0 = q_base + pl.program_id(2) * tq
    t0f = t0.astype(F32)
    ncp = kc_ref.shape[2]
    slopes = [jnp.where(g == 0, 2.0 ** -(h + 1), 2.0 ** -(GROUP + h + 1)).astype(F32) for h in range(GROUP)]
    hs = [slice(h * tq, (h + 1) * tq) for h in range(GROUP)]
    q = q_ref[0]
    qt = jnp.concatenate([q[:, h * LANE:(h + 1) * LANE].astype(F32).T for h in range(GROUP)], axis=1).astype(BF16)
    lane_t = t0 + lax.broadcasted_iota(jnp.int32, (1, tq), 1)

    sc = _dot(kc_ref[0, 0], qt)
    cpos = cpos_ref[...]
    valid_c = cpos <= lane_t.astype(F32)
    dist = cpos - t0f
    imp = jnp.zeros((ncp, tq), F32)
    pn_parts = []
    for h in range(GROUP):
        s = jnp.where(valid_c, sc[:, hs[h]] + slopes[h] * dist, NEG)
        m = jnp.max(s, axis=0, keepdims=True)
        p = jnp.where(valid_c, jnp.exp(s - m), 0.0)
        pn = p * (1.0 / jnp.maximum(jnp.sum(p, axis=0, keepdims=True), 1e-30))
        imp = imp + pn
        pn_parts.append(pn.astype(BF16))
    oc = _dot(kct_ref[0, 0], jnp.concatenate(pn_parts, axis=1))

    _select_blocks(imp, sh_s, sel_s, lane_t)

    m_s[...] = jnp.full(m_s.shape, NEG, F32)
    l_s[...] = jnp.zeros(l_s.shape, F32)
    acc_s[...] = jnp.zeros(acc_s.shape, F32)
    rowf = lax.broadcasted_iota(jnp.int32, (tk, tq), 0).astype(F32)
    for h in range(GROUP):
        bias_s[h] = slopes[h] * rowf
    blocks = tk // SEL_BLOCK

    def key_tile(kt, causal):
        k0 = pl.multiple_of(kt * tk, tk)
        st = _dot(ks_ref[0, 0, pl.ds(k0, tk), :], qt)
        selm = jnp.concatenate(
            [jnp.broadcast_to(sel_s[pl.ds(kt * blocks + j, 1), :], (SEL_BLOCK, tq)) for j in range(blocks)], axis=0)
        if causal:
            dtab = lax.broadcasted_iota(jnp.int32, (tk, tq), 0) - lax.broadcasted_iota(jnp.int32, (tk, tq), 1)
            selm = jnp.where(dtab <= t0 - k0, selm, 0.0)
        valid = selm > 0.5
        shift = (k0 - t0).astype(F32)
        p_parts, a_parts = [], []
        for h in range(GROUP):
            c = slopes[h] * shift
            s = jnp.where(valid, st[:, hs[h]] + bias_s[h], NEG)
            m_old = m_s[:, hs[h]]
            m_new = jnp.maximum(m_old, jnp.max(s, axis=0, keepdims=True) + c)
            a = jnp.exp(m_old - m_new)
            p = jnp.exp(s - (m_new - c))
            l_s[:, hs[h]] = a * l_s[:, hs[h]] + jnp.sum(p, axis=0, keepdims=True)
            m_s[:, hs[h]] = m_new
            p_parts.append(p.astype(BF16))
            a_parts.append(a)
        pv = _dot(kst_ref[0, 0, :, pl.ds(k0, tk)], jnp.concatenate(p_parts, axis=1))
        acc_s[...] = acc_s[...] * jnp.concatenate(a_parts, axis=1) + pv

    last = (t0 + tq - 1) // tk

    def maybe_tile(kt, carry):
        @pl.when(jnp.max(sel_s[pl.ds(kt * blocks, blocks), :]) > 0.5)
        def _():
            key_tile(kt, False)
        return carry

    lax.fori_loop(0, last, maybe_tile, 0)
    key_tile(last, True)
    osel = acc_s[...] * (1.0 / l_s[...])

    kr0 = pl.multiple_of(jnp.maximum(t0 - WINDOW - win_base, 0), LANE)
    sw = _dot(kw_ref[0, 0, pl.ds(kr0, WIN_ROWS), :], qt)
    wrow = lax.broadcasted_iota(jnp.int32, (WIN_ROWS, tq), 0)
    dw = (lane_t - win_base - kr0) - wrow
    valid_w = (dw >= 0) & (dw < WINDOW)
    relw = wrow.astype(F32) + (win_base + kr0 - t0).astype(F32)
    pw_parts = []
    for h in range(GROUP):
        s = jnp.where(valid_w, sw[:, hs[h]] + slopes[h] * relw, NEG)
        m = jnp.max(s, axis=0, keepdims=True)
        p = jnp.where(valid_w, jnp.exp(s - m), 0.0)
        pn = p * (1.0 / jnp.maximum(jnp.sum(p, axis=0, keepdims=True), 1e-30))
        pw_parts.append(pn.astype(BF16))
    ow = _dot(kwt_ref[0, 0, :, pl.ds(kr0, WIN_ROWS)], jnp.concatenate(pw_parts, axis=1))

    gt_s[...] = ng_ref[0].T
    o_parts = []
    for h in range(GROUP):
        grow = 3 * (GROUP * g + h)
        o_parts.append(gt_s[pl.ds(grow, 1), :] * oc[HEAD_DIM:, hs[h]]
                       + gt_s[pl.ds(grow + 1, 1), :] * osel[HEAD_DIM:, hs[h]]
                       + gt_s[pl.ds(grow + 2, 1), :] * ow[HEAD_DIM:, hs[h]])
    o_ref[0] = jnp.concatenate(o_parts, axis=0).T.astype(o_ref.dtype)


def _cmp_positions(n_cmp_rows, n_valid):
    p = np.arange(n_cmp_rows)
    c = RATIO * (p % (n_cmp_rows // RATIO)) + p // (n_cmp_rows // RATIO)
    pos = np.where(c < n_valid, c * CMP_STRIDE + CMP_BLOCK - 1, 1e9).astype(np.float32)
    return jnp.asarray(np.broadcast_to(pos[:, None], (n_cmp_rows, LANE)))


def _ratio_major(kc):
    nb, ncp, c = kc.shape
    return kc.reshape(nb, ncp // RATIO, RATIO, c).transpose(0, 2, 1, 3).reshape(nb, ncp, c)


def _attention(q, ng, kc, ks, kw, *, n_cmp_valid, q_base, win_base):
    nb, lq, _ = q.shape
    ncp, tkp, wr = kc.shape[1], ks.shape[1], kw.shape[1]
    nsp = tkp // SEL_BLOCK
    heads = lambda x: x.reshape(x.shape[0], x.shape[1], N_KV, LANE).transpose(0, 2, 1, 3)
    kc, ks, kw = heads(_ratio_major(kc)), heads(ks), heads(kw)
    tr = lambda x: jnp.swapaxes(x, 2, 3)
    cpos = _cmp_positions(ncp, n_cmp_valid)
    r = GROUP * ATT_TQ
    res = lambda rows, cols: pl.BlockSpec((1, 1, rows, cols), lambda b, g, i: (b, g, 0, 0), pipeline_mode=pl.Buffered(1))
    return pl.pallas_call(
        functools.partial(_attn_kernel, q_base=q_base, win_base=win_base),
        grid=(nb, N_KV, lq // ATT_TQ),
        in_specs=[pl.BlockSpec((1, ATT_TQ, GROUP * LANE), lambda b, g, i: (b, i, g)),
                  pl.BlockSpec((1, ATT_TQ, LANE), lambda b, g, i: (b, i, 0)),
                  res(ncp, LANE), res(LANE, ncp),
                  pl.BlockSpec((ncp, LANE), lambda b, g, i: (0, 0), pipeline_mode=pl.Buffered(1)),
                  res(tkp, LANE), res(LANE, tkp), res(wr, LANE), res(LANE, wr)],
        out_specs=pl.BlockSpec((1, ATT_TQ, GROUP * HEAD_DIM), lambda b, g, i: (b, i, g)),
        out_shape=jax.ShapeDtypeStruct((nb, lq, Q_W), BF16),
        scratch_shapes=[pltpu.VMEM((1, r), F32), pltpu.VMEM((1, r), F32), pltpu.VMEM((LANE, r), F32),
                        pltpu.VMEM((nsp, ATT_TQ), F32), pltpu.VMEM((ncp // RATIO + SUBLANE, ATT_TQ), F32),
                        pltpu.VMEM((LANE, ATT_TQ), F32), pltpu.VMEM((GROUP, ATT_TK, ATT_TQ), F32)],
        compiler_params=_cparams("arbitrary", "arbitrary", "arbitrary"),
        name="nsa_attention",
    )(q, ng, kc, tr(kc), cpos, ks, tr(ks), kw, tr(kw))


def _sample_attn_kernel(pt_ref, cache_ref, q2_ref, q4_ref, gate_ref, lt_ref, slope_ref, gsum_ref, kc_ref, kct_ref,
                        cpos_ref, win_ref, ksn_ref, vsn_ref, kwn_ref, vwn_ref, o_ref,
                        pbuf, m_s, l_s, acc_s, sel_s, sh_s, sem, *, n_pages, n_q):
    past = n_pages * PAGE_SIZE
    slot = _pages_pipeline(cache_ref, pt_ref, pbuf, sem, n_pages)
    lane_t = lt_ref[0:1, :]
    tf = lane_t.astype(F32)
    tq = lane_t - past
    slope = slope_ref[0:1, :]
    q2 = q2_ref[0]

    def softmax_update(s_tiles, v_tiles):
        m_old = m_s[...]
        m_new = m_old
        for s in s_tiles:
            m_new = jnp.maximum(m_new, jnp.max(s, axis=0, keepdims=True))
        a = jnp.exp(m_old - m_new)
        l_new = a * l_s[...]
        acc = acc_s[...] * a
        for s, v in zip(s_tiles, v_tiles):
            p = jnp.exp(s - m_new)
            l_new = l_new + jnp.sum(p, axis=0, keepdims=True)
            acc = acc + _dot(v, p.astype(BF16))
        l_s[...] = l_new
        m_s[...] = m_new
        acc_s[...] = acc

    cpos = cpos_ref[...]
    valid_c = cpos <= tf
    s = jnp.where(valid_c, _dot(kc_ref[0], q4_ref[0]) + slope * (cpos - tf), NEG)
    m = jnp.max(s, axis=0, keepdims=True)
    p = jnp.where(valid_c, jnp.exp(s - m), 0.0)
    pn = p * (1.0 / jnp.maximum(jnp.sum(p, axis=0, keepdims=True), 1e-30))
    oc = _dot(kct_ref[0], pn.astype(BF16))
    imp = jnp.dot(pn, gsum_ref[...], preferred_element_type=F32, precision=lax.Precision.HIGHEST)
    _select_blocks(imp, sh_s, sel_s, lane_t)

    m_s[...] = jnp.full(m_s.shape, NEG, F32)
    l_s[...] = jnp.zeros(l_s.shape, F32)
    acc_s[...] = jnp.zeros(acc_s.shape, F32)
    rowi = lax.broadcasted_iota(jnp.int32, (PAGE_SIZE, LANE), 0)
    rowf = rowi.astype(F32)
    blocks = PAGE_SIZE // SEL_BLOCK

    def page_group(i, carry):
        p0 = i * PAGE_UNROLL

        @pl.when(jnp.max(sel_s[pl.ds(p0 * blocks, PAGE_UNROLL * blocks), :]) > 0.5)
        def _():
            s_tiles, v_tiles = [], []
            for j in range(PAGE_UNROLL):
                p = p0 + j
                st = lax.dot_general(pbuf[slot, p, 0:KV_W, :].astype(BF16), q2, TN_DIMS, preferred_element_type=F32)
                selm = jnp.concatenate(
                    [jnp.broadcast_to(sel_s[pl.ds(p * blocks + k, 1), :], (SEL_BLOCK, LANE)) for k in range(blocks)],
                    axis=0)
                pos = rowf + lax.convert_element_type(p * PAGE_SIZE, F32)
                s_tiles.append(jnp.where(selm > 0.5, st + slope * (pos - tf), NEG))
                v_tiles.append(pbuf[slot, p, KV_W:PAGE_ROWS, :].astype(BF16))
            softmax_update(s_tiles, v_tiles)
        return carry

    lax.fori_loop(0, n_pages // PAGE_UNROLL, page_group, 0)
    new_ok = rowi <= tq
    cur_sel = sel_s[pl.ds(n_pages * blocks, 1), :] > 0.5
    s = jnp.where(new_ok, jnp.where(cur_sel, _dot(ksn_ref[0], q2) + slope * (rowf + (past - tf)), NEG), NEG)
    softmax_update([s], [vsn_ref[0]])
    osel = acc_s[...] * (1.0 / jnp.maximum(l_s[...], 1e-30))

    nbuf = win_ref.shape[2]
    wrow = lax.broadcasted_iota(jnp.int32, (nbuf, LANE), 0)
    sw = lax.dot_general(win_ref[0, 0:KV_W, :].astype(BF16), q2, TN_DIMS, preferred_element_type=F32)
    dw = (tq + nbuf) - wrow
    s_old = jnp.where(dw < WINDOW, sw - slope * dw.astype(F32), NEG)
    s_new = jnp.where(new_ok, _dot(kwn_ref[0], q2) + slope * (rowf + (past - tf)), NEG)
    m = jnp.maximum(jnp.max(s_old, axis=0, keepdims=True), jnp.max(s_new, axis=0, keepdims=True))
    p_old = jnp.exp(s_old - m)
    p_new = jnp.exp(s_new - m)
    inv = 1.0 / (jnp.sum(p_old, axis=0, keepdims=True) + jnp.sum(p_new, axis=0, keepdims=True))
    ow = (_dot(win_ref[0, KV_W:PAGE_ROWS, :].astype(BF16), (p_old * inv).astype(BF16))
          + _dot(vwn_ref[0], (p_new * inv).astype(BF16)))

    head0 = lax.broadcasted_iota(jnp.int32, (HEAD_DIM, LANE), 1) < GROUP * n_q
    pick = lambda x, off: jnp.where(head0, x[off:off + HEAD_DIM], x[off + x.shape[0] // 2:off + x.shape[0] // 2 + HEAD_DIM])
    o_ref[0] = (gate_ref[0, 0:1, :] * pick(oc, HEAD_DIM) + gate_ref[0, 1:2, :] * pick(osel, 0)
                + gate_ref[0, 2:3, :] * pick(ow, 0))


def _sample_attention(q, ng, kv_new, kc, cache_sel, win_state, page_table):
    nb, lq, _ = q.shape
    n_pages = page_table.shape[1]
    past = n_pages * PAGE_SIZE
    ncp = kc.shape[1]
    nbuf = win_state.shape[2]
    lanes = N_KV * GROUP * lq
    assert lanes <= LANE // 2 and nbuf == WINDOW and lq <= SUBLANE
    nsp = -(-(past // SEL_BLOCK + 1) // SUBLANE) * SUBLANE

    qh = q.reshape(nb, lq, N_KV, GROUP, 2, HEAD_DIM)[..., 0, :].astype(F32).transpose(0, 2, 3, 1, 4)
    eye = jnp.eye(N_KV, dtype=F32)
    lane_pad = lambda x: jnp.pad(x.reshape(nb, -1, lanes), ((0, 0), (0, 0), (0, LANE - lanes))).astype(BF16)
    q2 = lane_pad(jnp.einsum('bghtd,gk->bkdght', qh, eye))
    q4 = lane_pad(jnp.einsum('bghtd,gk,s->bksdght', qh, eye, jnp.array([1.0, 0.0], F32)))
    gates = ng[:, :, :3 * N_HEADS].reshape(nb, lq, N_KV, GROUP, 3).transpose(0, 4, 2, 3, 1).reshape(nb, 3, lanes)
    gates = jnp.pad(gates, ((0, 0), (0, SUBLANE - 3), (0, LANE - lanes)))
    lane = np.arange(LANE)
    real = lane < lanes
    lane_g, lane_h, lane_q = lane // (GROUP * lq), (lane // lq) % GROUP, lane % lq
    lt = jnp.asarray(np.broadcast_to(np.where(real, past + lane_q, past).astype(np.int32), (SUBLANE, LANE)))
    slope = jnp.asarray(np.broadcast_to(np.where(real, 2.0 ** -(GROUP * lane_g + lane_h + 1.0), 1.0)
                                        .astype(np.float32), (SUBLANE, LANE)))
    same = (lane_g[:, None] == lane_g[None, :]) & (lane_q[:, None] == lane_q[None, :]) & real[:, None] & real[None, :]
    gsum = jnp.asarray(same.astype(np.float32))
    kcp = _ratio_major(kc)
    cpos = _cmp_positions(ncp, past // CMP_STRIDE - 1)
    new_rows = lambda c: jnp.pad(kv_new[:, :, c * KV_W:(c + 1) * KV_W], ((0, 0), (0, LANE - lq), (0, 0))).astype(BF16)
    ksn, vsn, kwn, vwn = new_rows(2), jnp.swapaxes(new_rows(3), 1, 2), new_rows(4), jnp.swapaxes(new_rows(5), 1, 2)

    per_b = lambda *shape: pl.BlockSpec((1,) + shape, lambda b, pt: (b,) + (0,) * len(shape))
    const = lambda arr: pl.BlockSpec(arr.shape, lambda b, pt: (0,) * arr.ndim, pipeline_mode=pl.Buffered(1))
    o = pl.pallas_call(
        functools.partial(_sample_attn_kernel, n_pages=n_pages, n_q=lq),
        grid_spec=pltpu.PrefetchScalarGridSpec(
            num_scalar_prefetch=1,
            grid=(nb,),
            in_specs=[pl.BlockSpec(memory_space=pl.ANY),
                      per_b(KV_W, LANE), per_b(PAGE_ROWS, LANE), per_b(SUBLANE, LANE),
                      const(lt), const(slope), const(gsum),
                      per_b(ncp, PAGE_ROWS), per_b(PAGE_ROWS, ncp), const(cpos),
                      per_b(PAGE_ROWS, nbuf), per_b(LANE, KV_W), per_b(KV_W, LANE), per_b(LANE, KV_W), per_b(KV_W, LANE)],
            out_specs=per_b(HEAD_DIM, LANE),
            scratch_shapes=[pltpu.VMEM((2, n_pages, PAGE_ROWS, PAGE_SIZE), F32),
                            pltpu.VMEM((1, LANE), F32), pltpu.VMEM((1, LANE), F32), pltpu.VMEM((KV_W, LANE), F32),
                            pltpu.VMEM((nsp, LANE), F32), pltpu.VMEM((ncp // RATIO + SUBLANE, LANE), F32),
                            pltpu.SemaphoreType.DMA((2,))]),
        out_shape=jax.ShapeDtypeStruct((nb, HEAD_DIM, LANE), F32),
        compiler_params=_cparams("arbitrary"),
        name="sample_attention",
    )(page_table, cache_sel, q2, q4, gates, lt, slope, gsum, kcp, jnp.swapaxes(kcp, 1, 2), cpos,
      win_state, ksn, vsn, kwn, vwn)
    o = o[:, :, :lanes].reshape(nb, HEAD_DIM, N_KV, GROUP, lq).transpose(0, 4, 2, 3, 1)
    return o.reshape(nb, lq, Q_W).astype(BF16)


HIST = 32


def _mix_kernel(x_ref, a_ref, hist_ref, o_ref, ga_ref, gb_ref, g1_ref, dw_ref, dwb_ref, cg_ref, cb_ref,
                wco_ref, wno_ref, wout_ref, l1g_ref, l1b_ref, x1_ref, ext_s, *, zero_first_hist):
    sb, lb, d = x_ref.shape
    tm = sb * lb
    hist = hist_ref[...]
    if zero_first_hist:
        hist = jnp.where(pl.program_id(1) > 0, hist, 0.0)
    ext_s[:, 0:HIST, :] = hist
    ext_s[:, HIST:HIST + lb, :] = a_ref[...]
    y = jnp.zeros((sb, lb, D_CONV), F32) + dwb_ref[...]
    for j in range(CONV_W):
        y = y + dw_ref[j:j + 1, :] * ext_s[:, pl.ds(j + HIST - (CONV_W - 1), lb), :]
    yn = _ln(y) * cg_ref[...] + cb_ref[...]
    act = (yn * _sigmoid(yn)).reshape(tm, D_CONV)
    out_a = _dot(act.astype(BF16), wco_ref[...])
    out_b = _dot(o_ref[...].reshape(tm, Q_W), wno_ref[...])
    merged = (ga_ref[...].reshape(tm, d).astype(F32) * out_a + gb_ref[...].reshape(tm, d).astype(F32) * out_b)
    mix = _dot(merged.astype(BF16), wout_ref[...]).reshape(sb, lb, d)
    x1_ref[...] = _ln(ALPHA * x_ref[...] + g1_ref[...] * mix) * l1g_ref[...] + l1b_ref[...]


def _mix(x, a, hist, o, ga, gb, mod, lw, tm, *, hist_from_a):
    nseq, seqlen, d = x.shape
    sb, lb, grid = _token_blocks(nseq, seqlen, tm)
    tok = lambda width: pl.BlockSpec((sb, lb, width), lambda s, t: (s, t, 0))
    if hist_from_a:
        hist_spec = pl.BlockSpec((sb, HIST, D_CONV), lambda s, t: (s, jnp.maximum(t * (lb // HIST) - 1, 0), 0))
    else:
        hist_spec = pl.BlockSpec((sb, HIST, D_CONV), lambda s, t: (s, 0, 0))
    const = lambda arr: pl.BlockSpec(arr.shape, lambda s, t: (0,) * arr.ndim, pipeline_mode=pl.Buffered(1))
    return pl.pallas_call(
        functools.partial(_mix_kernel, zero_first_hist=hist_from_a),
        grid=grid,
        in_specs=[tok(d), tok(D_CONV), hist_spec, tok(Q_W), tok(d), tok(d),
                  pl.BlockSpec((sb, 1, d), lambda s, t: (s, 0, 2))] + [const(w) for w in lw],
        out_specs=tok(d),
        out_shape=jax.ShapeDtypeStruct((nseq, seqlen, d), F32),
        scratch_shapes=[pltpu.VMEM((sb, HIST + lb, D_CONV), F32)],
        compiler_params=_cparams("parallel", "arbitrary"),
        name="mix",
    )(x, a, hist, o, ga, gb, mod, *lw)


NO_RANK = 99
CAND_ROWS = PEER_TOPK + SUBLANE * (PEER_TOPK - 1)


def _top_ranked(w, idx, n_top):
    rank = jnp.full(w.shape, NO_RANK, jnp.int32)
    vals = []
    for k in range(n_top):
        m = jnp.max(w, axis=0, keepdims=True)
        hit = idx == jnp.min(jnp.where(w == m, idx, BIG_I), axis=0, keepdims=True)
        w = jnp.where(hit, -jnp.inf, w)
        rank = jnp.where(hit, k, rank)
        vals.append(m)
    return rank, vals


def _peer_sel_kernel(x1_ref, sh_ref, sc_ref, wq_ref, keys_ref, h2t_ref, n0_ref, e0_ref, r1_ref, e1_ref, q_s):
    sb, lb, d = x1_ref.shape
    tm = sb * lb
    h2 = (_ln(x1_ref[...]) * (1.0 + sc_ref[...]) + sh_ref[...]).reshape(tm, d)
    h2t_ref[...] = h2.T.astype(BF16)
    q_s[...] = _dot(h2.astype(BF16), wq_ref[...]).astype(BF16)
    idx = lax.broadcasted_iota(jnp.int32, (N_KEYS, tm), 0)
    row8 = lax.broadcasted_iota(jnp.int32, (SUBLANE, tm), 0)
    cidx = lax.broadcasted_iota(jnp.int32, (CAND_ROWS, tm), 0)

    def head(h, carry):
        s0 = lax.dot_general(keys_ref[2 * h], q_s[:, pl.ds(pl.multiple_of(2 * h * N_KEYS, N_KEYS), N_KEYS)], NT_DIMS,
                             preferred_element_type=F32)
        s1 = lax.dot_general(keys_ref[2 * h + 1], q_s[:, pl.ds(pl.multiple_of((2 * h + 1) * N_KEYS, N_KEYS), N_KEYS)],
                             NT_DIMS, preferred_element_type=F32)
        rank0, v0 = _top_ranked(s0, idx, PEER_TOPK)
        rank1, v1 = _top_ranked(s1, idx, PEER_TOPK)
        v1a = jnp.concatenate(v1[:SUBLANE], axis=0)
        v1b = jnp.concatenate(v1[SUBLANE:], axis=0)
        pieces = [v1a + v0[0], v1b + v0[0]]
        for i in range(1, PEER_TOPK):
            pieces.append(jnp.where(row8 < PEER_TOPK // (i + 1), v1a + v0[i], -jnp.inf))
        cand = jnp.concatenate(pieces, axis=0)
        crank, _ = _top_ranked(cand, cidx, PEER_TOPK)
        chosen = crank < PEER_TOPK
        top = v0[0] + v1[0]
        z = jnp.sum(jnp.where(chosen, jnp.exp(cand - top), 0.0), axis=0, keepdims=True)
        cf = jnp.where(chosen, 1.0, 0.0)
        counts = [jnp.sum(cf[0:2 * SUBLANE], axis=0, keepdims=True)]
        for i in range(1, PEER_TOPK):
            counts.append(jnp.sum(cf[SUBLANE * (i + 1):SUBLANE * (i + 2)], axis=0, keepdims=True))
        n0 = jnp.zeros((N_KEYS, tm), F32)
        for i in range(PEER_TOPK):
            n0 = jnp.where(rank0 == i, counts[i], n0)
        n0_ref[h] = n0
        e0_ref[h] = jnp.where(rank0 < PEER_TOPK, jnp.exp(s0 - v0[0]), 0.0)
        r1_ref[h] = rank1.astype(F32).astype(BF16)
        e1_ref[h] = (jnp.where(rank1 < PEER_TOPK, jnp.exp(s1 - v1[0]), 0.0) * (1.0 / z)).astype(BF16)
        return carry

    lax.fori_loop(0, PEER_HEADS, head, 0)


def _peer_sel(x1, mod, wq, keys, tm):
    nseq, seqlen, d = x1.shape
    n = nseq * seqlen
    sb, lb, grid = _token_blocks(nseq, seqlen, tm)
    tm = sb * lb
    tok = pl.BlockSpec((sb, lb, d), lambda s, t: (s, t, 0))
    flat = lambda s, t: s * grid[1] + t
    stat = pl.BlockSpec((PEER_HEADS, N_KEYS, tm), lambda s, t: (0, 0, flat(s, t)))
    const = lambda arr: pl.BlockSpec(arr.shape, lambda s, t: (0,) * arr.ndim, pipeline_mode=pl.Buffered(1))
    stat_shape = jax.ShapeDtypeStruct((PEER_HEADS, N_KEYS, n), F32)
    slab_shape = jax.ShapeDtypeStruct((PEER_HEADS, N_KEYS, n), BF16)
    return pl.pallas_call(
        _peer_sel_kernel,
        grid=grid,
        in_specs=[tok, pl.BlockSpec((sb, 1, d), lambda s, t: (s, 0, 3)), pl.BlockSpec((sb, 1, d), lambda s, t: (s, 0, 4)),
                  const(wq), const(keys)],
        out_specs=[pl.BlockSpec((d, tm), lambda s, t: (0, flat(s, t))), stat, stat, stat, stat],
        out_shape=[jax.ShapeDtypeStruct((d, n), BF16), stat_shape, stat_shape, slab_shape, slab_shape],
        scratch_shapes=[pltpu.VMEM((tm, 2 * PEER_HEADS * N_KEYS), BF16)],
        compiler_params=_cparams("parallel", "parallel"),
        name="peer_select",
    )(x1, mod, mod, wq, keys)


PEER_CE = 1024


def _peer_ffn_kernel(h2t_ref, u_ref, vt_ref, n0_ref, e0_ref, r1_ref, e1_ref, x1_ref, g2_ref, l2g_ref, l2b_ref,
                     y_ref, acc_s, wg_s):
    c = pl.program_id(2)

    @pl.when(c == 0)
    def _():
        acc_s[...] = jnp.zeros(acc_s.shape, F32)

    ut = _dot(u_ref[...], h2t_ref[...])
    for e in range(PEER_CE // N_KEYS):
        rows = slice(e * N_KEYS, (e + 1) * N_KEYS)
        w = jnp.zeros((N_KEYS, ut.shape[1]), BF16)
        for h in range(PEER_HEADS):
            n0 = n0_ref[h, e:e + 1, :].astype(BF16)
            e0 = e0_ref[h, e:e + 1, :].astype(BF16)
            w = w + jnp.where(r1_ref[h] < n0, e1_ref[h], jnp.zeros((), BF16)) * e0
        wg_s[rows, :] = w * _gelu(ut[rows]).astype(BF16)
    acc_s[...] += _dot(vt_ref[...], wg_s[...])

    @pl.when(c == pl.num_programs(2) - 1)
    def _():
        sb, lb, d = x1_ref.shape
        f = acc_s[...].T.reshape(sb, lb, d)
        y_ref[...] = _ln(ALPHA * x1_ref[...] + g2_ref[...] * f) * l2g_ref[...] + l2b_ref[...]


def _peer_ffn(h2t, u, vt, stats, x1, mod, l2g, l2b, tm):
    nseq, seqlen, d = x1.shape
    sb, lb, grid = _token_blocks(nseq, seqlen, tm)
    tm = sb * lb
    n_exp = u.shape[0]
    flat = lambda s, t: s * grid[1] + t
    tok = pl.BlockSpec((sb, lb, d), lambda s, t, c: (s, t, 0))
    c1 = PEER_CE // N_KEYS
    row_stat = pl.BlockSpec((PEER_HEADS, c1, tm), lambda s, t, c: (0, c, flat(s, t)))
    slab_stat = pl.BlockSpec((PEER_HEADS, N_KEYS, tm), lambda s, t, c: (0, 0, flat(s, t)))
    vec = pl.BlockSpec((1, d), lambda s, t, c: (0, 0))
    n0, e0, r1, e1 = stats
    return pl.pallas_call(
        _peer_ffn_kernel,
        grid=grid + (n_exp // PEER_CE,),
        in_specs=[pl.BlockSpec((d, tm), lambda s, t, c: (0, flat(s, t))),
                  pl.BlockSpec((PEER_CE, d), lambda s, t, c: (c, 0)),
                  pl.BlockSpec((d, PEER_CE), lambda s, t, c: (0, c)),
                  row_stat, row_stat, slab_stat, slab_stat, tok,
                  pl.BlockSpec((sb, 1, d), lambda s, t, c: (s, 0, 5)), vec, vec],
        out_specs=tok,
        out_shape=jax.ShapeDtypeStruct((nseq, seqlen, d), F32),
        scratch_shapes=[pltpu.VMEM((d, tm), F32), pltpu.VMEM((PEER_CE, tm), BF16)],
        compiler_params=_cparams("parallel", "parallel", "arbitrary"),
        name="peer_experts",
    )(h2t, u, vt, n0, e0, r1, e1, x1, mod, l2g, l2b)


TOKEN_TILE = 512
PEER_SEL_TILE = 256


def _channel_mixing(x, mod, a, o, ga, gb, hist, lw_mix, peer, *, hist_from_a):
    x1 = _mix(x, a, hist, o, ga, gb, mod, lw_mix, TOKEN_TILE, hist_from_a=hist_from_a)
    wq, keys, u, vt, l2g, l2b = peer
    h2t, *stats = _peer_sel(x1, mod, wq, keys, PEER_SEL_TILE)
    return _peer_ffn(h2t, u, vt, stats, x1, mod, l2g, l2b, TOKEN_TILE)


def kernel(x_prompt, x_sample, cache_cmp_kv, cache_sel_kv, state_win_kv, state_conv, page_table, c_prompt, c_sample,
           w_ada, b_ada, w_in, b_in, conv_dw, conv_dw_b, conv_ln_g, conv_ln_b, w_conv_out, cmp_pos, w_cmp_k1, w_cmp_k2,
           w_cmp_v1, w_cmp_v2, w_nsa_out, w_out, ln1_g, ln1_b, peer_wq, peer_keys, peer_u, peer_v, ln2_g, ln2_b):
    assert w_ada.shape[0] == DEPTH == 1
    l = 0
    nbp, s, d = x_prompt.shape
    nbs, ls, _ = x_sample.shape
    n_pages = page_table.shape[1]
    past = n_pages * PAGE_SIZE
    nbuf = state_win_kv.shape[2]
    n_phys = cache_cmp_kv.shape[1]
    assert s % (CMP_ROWS * CMP_STRIDE) == 0 and past % (CMP_ROWS * CMP_STRIDE) == 0 and nbuf == WINDOW
    assert n_pages % PAGE_UNROLL == 0

    w_ext, b_ext = _extend_w_in(w_in[l], b_in[l])
    cw = _compress_weights(cmp_pos[l], w_cmp_k1[l], w_cmp_k2[l], w_cmp_v1[l], w_cmp_v2[l])
    row = lambda v: v.reshape(1, -1).astype(F32)
    lw_mix = (jnp.pad(conv_dw[l], ((0, HIST - CONV_W), (0, 0))), row(conv_dw_b[l]), row(conv_ln_g[l]), row(conv_ln_b[l]),
              w_conv_out[l].astype(BF16), w_nsa_out[l].astype(BF16), w_out[l].astype(BF16), row(ln1_g[l]), row(ln1_b[l]))
    peer = (peer_wq[l].astype(BF16), peer_keys[l].reshape(2 * PEER_HEADS, N_KEYS, -1).astype(BF16),
            peer_u[l].astype(BF16), peer_v[l].T.astype(BF16), row(ln2_g[l]), row(ln2_b[l]))

    c_all = jnp.concatenate([c_prompt, c_sample], axis=0)
    c_all = jnp.pad(c_all, ((0, (-c_all.shape[0]) % SUBLANE), (0, 0)))
    mod = _adaln(c_all, w_ada[l], b_ada[l])
    mod_p = mod[:nbp].reshape(nbp, 1, -1)
    mod_s = mod[nbp:nbp + nbs].reshape(nbs, 1, -1)

    a_p, q_p, kv_p, kvg_p, ng_p, ga_p, gb_p = _inproj(x_prompt, mod_p, w_ext, b_ext, TOKEN_TILE)
    kc_p = _compress_prompt(kv_p[:, :, :2 * KV_W].reshape(nbp, 2 * s, LANE), cw)
    o_p = _attention(q_p, ng_p, kc_p, kvg_p[:, :, 2 * KV_W:4 * KV_W], kvg_p[:, :, 4 * KV_W:],
                     n_cmp_valid=s // CMP_STRIDE - 1, q_base=0, win_base=0)
    y_p = _channel_mixing(x_prompt, mod_p, a_p, o_p, ga_p, gb_p, a_p, lw_mix, peer, hist_from_a=True)

    a_s, q_s, kv_s, kvg_s, ng_s, ga_s, gb_s = _inproj(x_sample, mod_s, w_ext, b_ext, TOKEN_TILE)
    pages = lambda cache: cache[l].transpose(0, 2, 3, 4, 1).reshape(n_phys, PAGE_ROWS, PAGE_SIZE)
    kc_s = _compress_sample(pages(cache_cmp_kv), page_table, cw)
    win_t = state_win_kv[l].transpose(0, 2, 3, 4, 1).reshape(nbs, PAGE_ROWS, nbuf)
    o_s = _sample_attention(q_s, ng_s, kv_s, kc_s, pages(cache_sel_kv), win_t, page_table)
    hist_s = jnp.pad(state_conv[l], ((0, 0), (HIST - (CONV_W - 1), 0), (0, 0)))
    y_s = _channel_mixing(x_sample, mod_s, a_s, o_s, ga_s, gb_s, hist_s, lw_mix, peer, hist_from_a=False)

    kv6 = lambda t, c: t[:, :, c * 2 * KV_W:(c + 1) * 2 * KV_W].reshape(1, t.shape[0], t.shape[1], 2, N_KV, HEAD_DIM)
    nb = min(WINDOW, s)
    win_new = jnp.concatenate([state_win_kv[l], kv6(kv_s, 2)[0]], axis=1)[:, ls:]
    conv_new = jnp.concatenate([state_conv[l], a_s], axis=1)[:, ls:]
    return (y_p, y_s, kv6(kv_p, 0), kv6(kv_p, 1), kv6(kv_p, 2)[:, :, s - nb:], a_p[None, :, s - (CONV_W - 1):],
            kv6(kv_s, 0), kv6(kv_s, 1), win_new[None], conv_new[None])
```

```python
import functools

import numpy as np
import jax
import jax.numpy as jnp
from jax import lax
from jax.experimental import pallas as pl
from jax.experimental.pallas import tpu as pltpu

F32 = jnp.float32
BF16 = jnp.bfloat16

D_MODEL = 1024
D_CONV = 512
CONV_W = 31
N_HEADS = 8
HEAD_DIM = 64
N_KV = 2
GROUP = N_HEADS // N_KV
CMP_BLOCK = 32
CMP_STRIDE = 16
CMP_HID = 256
SEL_BLOCK = 64
SEL_TOP = 16
RATIO = SEL_BLOCK // CMP_STRIDE
WINDOW = 512
PAGE_SIZE = 128
PEER_HEADS = 8
N_KEYS = 128
PEER_TOPK = 16
DEPTH = 1
ALPHA = (2.0 * DEPTH) ** 0.25
LN_EPS = 1e-5
Q_W = N_HEADS * HEAD_DIM
KV_W = N_KV * HEAD_DIM

LANE = 128
SUBLANE = 8
VMEM_LIMIT = 56 * 1024 * 1024
NEG = -1e30
BIG_I = 1 << 20

SEG_A = 2 * D_CONV
SEG_Q = N_HEADS * LANE
SEG_KV = 6 * KV_W
SEG_KVG = 6 * KV_W
SEG_G = LANE
SEG_M = 2 * D_MODEL
OFF_A = 0
OFF_Q = OFF_A + SEG_A
OFF_KV = OFF_Q + SEG_Q
OFF_KVG = OFF_KV + SEG_KV
OFF_G = OFF_KVG + SEG_KVG
OFF_M = OFF_G + SEG_G
W_EXT = OFF_M + SEG_M

TN_DIMS = (((0,), (0,)), ((), ()))
NT_DIMS = (((1,), (1,)), ((), ()))


def _cparams(*sem):
    return pltpu.CompilerParams(dimension_semantics=sem, vmem_limit_bytes=VMEM_LIMIT)


def _ln(x):
    mu = jnp.mean(x, axis=-1, keepdims=True)
    xc = x - mu
    return xc * lax.rsqrt(jnp.mean(xc * xc, axis=-1, keepdims=True) + LN_EPS)


def _sigmoid(x):
    return 1.0 / (1.0 + jnp.exp(-x))


def _gelu(x):
    return 0.5 * x * (1.0 + jnp.tanh(0.7978845608028654 * (x + 0.044715 * (x * x * x))))


def _dot(a, b):
    return jnp.dot(a, b, preferred_element_type=F32)


def _adaln_kernel(c_ref, w_ref, b_ref, o_ref):
    c = c_ref[...]
    s = c * _sigmoid(c)
    o_ref[...] = _dot(s.astype(BF16), w_ref[...].astype(BF16)) + b_ref[...]


def _adaln(c, w, b):
    rows, d = c.shape
    n = w.shape[1]
    tn = 1024
    return pl.pallas_call(
        _adaln_kernel,
        grid=(n // tn,),
        in_specs=[pl.BlockSpec((rows, d), lambda j: (0, 0)),
                  pl.BlockSpec((d, tn), lambda j: (0, j)),
                  pl.BlockSpec((1, tn), lambda j: (0, j))],
        out_specs=pl.BlockSpec((rows, tn), lambda j: (0, j)),
        out_shape=jax.ShapeDtypeStruct((rows, n), F32),
        compiler_params=_cparams("parallel"),
        name="adaln",
    )(c, w, b.reshape(1, n))


def _inproj_kernel(x_ref, mod_ref, w_ref, b_ref, a_ref, q_ref, kv_ref, kvg_ref, ng_ref, ga_ref, gb_ref):
    sb, lb, d = x_ref.shape
    tm = sb * lb
    mod = mod_ref[...]
    h = _ln(x_ref[...]) * (1.0 + mod[:, :, d:2 * d]) + mod[:, :, 0:d]
    hb = h.reshape(tm, d).astype(BF16)

    def seg(off, width):
        return _dot(hb, w_ref[:, off:off + width]) + b_ref[:, off:off + width]

    z = seg(OFF_A, SEG_A)
    a_ref[...] = (z[:, :D_CONV] * _sigmoid(z[:, D_CONV:])).reshape(sb, lb, D_CONV)
    q_ref[...] = seg(OFF_Q, SEG_Q).astype(BF16).reshape(sb, lb, SEG_Q)
    kv_ref[...] = seg(OFF_KV, SEG_KV).reshape(sb, lb, SEG_KV)
    kvg_ref[...] = seg(OFF_KVG, SEG_KVG).astype(BF16).reshape(sb, lb, SEG_KVG)
    ng_ref[...] = _sigmoid(seg(OFF_G, SEG_G)).reshape(sb, lb, SEG_G)
    ga_ref[...] = _sigmoid(seg(OFF_M, D_MODEL)).astype(BF16).reshape(sb, lb, D_MODEL)
    gb_ref[...] = _sigmoid(seg(OFF_M + D_MODEL, D_MODEL)).astype(BF16).reshape(sb, lb, D_MODEL)


def _token_blocks(nseq, seqlen, tm):
    if seqlen >= tm:
        return 1, tm, (nseq, seqlen // tm)
    sb = min(tm // seqlen, nseq)
    return sb, seqlen, (nseq // sb, 1)


def _inproj(x, mod, w_ext, b_ext, tm):
    nseq, seqlen, d = x.shape
    sb, lb, grid = _token_blocks(nseq, seqlen, tm)
    tok = lambda width: pl.BlockSpec((sb, lb, width), lambda s, t: (s, t, 0))
    widths = (D_CONV, SEG_Q, SEG_KV, SEG_KVG, SEG_G, D_MODEL, D_MODEL)
    dtypes = (F32, BF16, F32, BF16, F32, BF16, BF16)
    return pl.pallas_call(
        _inproj_kernel,
        grid=grid,
        in_specs=[tok(d),
                  pl.BlockSpec((sb, 1, 2 * d), lambda s, t: (s, 0, 0)),
                  pl.BlockSpec((d, W_EXT), lambda s, t: (0, 0), pipeline_mode=pl.Buffered(1)),
                  pl.BlockSpec((1, W_EXT), lambda s, t: (0, 0), pipeline_mode=pl.Buffered(1))],
        out_specs=[tok(w) for w in widths],
        out_shape=[jax.ShapeDtypeStruct((nseq, seqlen, w), dt) for w, dt in zip(widths, dtypes)],
        compiler_params=_cparams("parallel", "parallel"),
        name="inproj",
    )(x, mod, w_ext, b_ext)


def _extend_w_in(w_in, b_in):
    scale = HEAD_DIM ** -0.5
    c_q = 2 * D_CONV
    c_kv = c_q + Q_W
    c_g = c_kv + 6 * KV_W
    c_m = c_g + 3 * N_HEADS
    cols_w, cols_b = [], []

    def add(w, b):
        cols_w.append(w)
        cols_b.append(b)

    add(w_in[:, :c_q], b_in[:c_q])
    zw = jnp.zeros((w_in.shape[0], HEAD_DIM), w_in.dtype)
    zb = jnp.zeros((HEAD_DIM,), b_in.dtype)
    for h in range(N_HEADS):
        sl = slice(c_q + h * HEAD_DIM, c_q + (h + 1) * HEAD_DIM)
        add(w_in[:, sl] * scale, b_in[sl] * scale)
        add(zw, zb)
    add(w_in[:, c_kv:c_g], b_in[c_kv:c_g])
    for br in range(3):
        k0 = c_kv + br * 2 * KV_W
        v0 = k0 + KV_W
        for g in range(N_KV):
            ks = slice(k0 + g * HEAD_DIM, k0 + (g + 1) * HEAD_DIM)
            vs = slice(v0 + g * HEAD_DIM, v0 + (g + 1) * HEAD_DIM)
            add(w_in[:, ks], b_in[ks])
            add(w_in[:, vs], b_in[vs])
    add(w_in[:, c_g:c_m], b_in[c_g:c_m])
    add(jnp.zeros((w_in.shape[0], SEG_G - 3 * N_HEADS), w_in.dtype), jnp.zeros((SEG_G - 3 * N_HEADS,), b_in.dtype))
    add(w_in[:, c_m:], b_in[c_m:])
    w = jnp.concatenate(cols_w, axis=1).astype(BF16)
    b = jnp.concatenate(cols_b, axis=0).reshape(1, -1).astype(F32)
    assert w.shape[1] == W_EXT
    return w, b


CMP_ROWS = 256


def _compress_rows(load_k, load_v, pos_ref, w1k_ref, w1v_ref, w2_ref):
    acc_k = jnp.zeros((CMP_ROWS, 2 * CMP_HID), F32)
    acc_v = jnp.zeros((CMP_ROWS, 2 * CMP_HID), F32)
    for j in range(0, CMP_BLOCK, 2):
        pos = [pos_ref[j + i:j + i + 1, :] for i in range(2)]
        xk = jnp.concatenate([(load_k(j + i) + pos[i]).astype(BF16) for i in range(2)], axis=1)
        xv = jnp.concatenate([(load_v(j + i) + pos[i]).astype(BF16) for i in range(2)], axis=1)
        acc_k = acc_k + _dot(xk, w1k_ref[j // 2])
        acc_v = acc_v + _dot(xv, w1v_ref[j // 2])
    hk = _gelu(acc_k)
    hv = _gelu(acc_v)
    outs = []
    for g in range(N_KV):
        hg = jnp.concatenate([hk[:, g * CMP_HID:(g + 1) * CMP_HID], hv[:, g * CMP_HID:(g + 1) * CMP_HID]], axis=1)
        outs.append(_dot(hg.astype(BF16), w2_ref[...]))
    return jnp.concatenate(outs, axis=1)


def _compress_weights(cmp_pos, w_k1, w_k2, w_v1, w_v2):
    def bd1(w1):
        w = w1.reshape(CMP_BLOCK, HEAD_DIM, CMP_HID)
        z = jnp.zeros_like(w)
        top = jnp.concatenate([w, z], axis=2)
        bot = jnp.concatenate([z, w], axis=2)
        wbd = jnp.concatenate([top, bot], axis=1)
        return wbd.reshape(CMP_BLOCK // 2, 2 * KV_W, 2 * CMP_HID).astype(BF16)
    zk = jnp.zeros_like(w_k2)
    w2 = jnp.concatenate([jnp.concatenate([w_k2, zk], axis=1),
                          jnp.concatenate([zk, w_v2], axis=1)], axis=0).astype(BF16)
    pos2 = jnp.tile(cmp_pos, (1, N_KV)).astype(F32)
    return pos2, bd1(w_k1), bd1(w_v1), w2


def _compress_prompt_kernel(x_ref, halo_ref, pos_ref, w1k_ref, w1v_ref, w2_ref, o_ref, xs_ref, *, n_valid):
    rows2 = x_ref.shape[1]
    xs_ref[0:rows2, :] = x_ref[0]
    xs_ref[rows2:rows2 + 2 * CMP_STRIDE, :] = halo_ref[0]
    n_rows = rows2 // (2 * CMP_STRIDE)
    first = pl.program_id(1) * n_rows
    for r in range(n_rows // CMP_ROWS):
        base = 2 * CMP_STRIDE * CMP_ROWS * r
        load = lambda off: (lambda j: xs_ref[pl.ds(base + 2 * j + off, CMP_ROWS, stride=2 * CMP_STRIDE), :])
        out = _compress_rows(load(0), load(1), pos_ref, w1k_ref, w1v_ref, w2_ref)
        idx = first + r * CMP_ROWS + lax.broadcasted_iota(jnp.int32, out.shape, 0)
        o_ref[0, r * CMP_ROWS:(r + 1) * CMP_ROWS, :] = jnp.where(idx < n_valid, out, 0.0).astype(o_ref.dtype)


def _compress_prompt(kvc, cw):
    nb, s2, _ = kvc.shape
    s = s2 // 2
    pos2, w1k, w1v, w2 = cw
    p = min(CMP_ROWS * CMP_STRIDE, s)
    n_ch = s // CMP_STRIDE
    rows = p // CMP_STRIDE
    const = lambda shape: pl.BlockSpec(shape, lambda b, i: (0,) * len(shape), pipeline_mode=pl.Buffered(1))
    return pl.pallas_call(
        functools.partial(_compress_prompt_kernel, n_valid=n_ch - 1),
        grid=(nb, s // p),
        in_specs=[pl.BlockSpec((1, 2 * p, LANE), lambda b, i: (b, i, 0)),
                  pl.BlockSpec((1, 2 * CMP_STRIDE, LANE), lambda b, i: (b, jnp.minimum((i + 1) * rows, n_ch - 1), 0)),
                  const(pos2.shape), const(w1k.shape), const(w1v.shape), const(w2.shape)],
        out_specs=pl.BlockSpec((1, rows, 2 * KV_W), lambda b, i: (b, i, 0)),
        out_shape=jax.ShapeDtypeStruct((nb, n_ch, 2 * KV_W), BF16),
        scratch_shapes=[pltpu.VMEM((2 * p + 2 * CMP_STRIDE, LANE), F32)],
        compiler_params=_cparams("parallel", "parallel"),
        name="compress_prompt",
    )(kvc, kvc, pos2, w1k, w1v, w2)


PAGE_ROWS = 2 * KV_W
PAGE_UNROLL = 4


def _page_copy(cache_ref, pt_ref, buf_ref, sem_ref, b, p, slot):
    return pltpu.make_async_copy(cache_ref.at[pt_ref[b, p]], buf_ref.at[slot, p], sem_ref.at[slot])


def _pages_start(cache_ref, pt_ref, buf_ref, sem_ref, b, slot, n_pages):
    def body(p, c):
        _page_copy(cache_ref, pt_ref, buf_ref, sem_ref, b, p, slot).start()
        return c
    lax.fori_loop(0, n_pages, body, 0)


def _pages_wait(cache_ref, pt_ref, buf_ref, sem_ref, b, slot, n_pages):
    def body(p, c):
        _page_copy(cache_ref, pt_ref, buf_ref, sem_ref, b, p, slot).wait()
        return c
    lax.fori_loop(0, n_pages, body, 0)


def _pages_pipeline(cache_ref, pt_ref, buf_ref, sem_ref, n_pages):
    b = pl.program_id(0)
    slot = lax.rem(b, 2)

    @pl.when(b == 0)
    def _():
        _pages_start(cache_ref, pt_ref, buf_ref, sem_ref, b, slot, n_pages)

    @pl.when(b + 1 < pl.num_programs(0))
    def _():
        _pages_start(cache_ref, pt_ref, buf_ref, sem_ref, b + 1, 1 - slot, n_pages)

    _pages_wait(cache_ref, pt_ref, buf_ref, sem_ref, b, slot, n_pages)
    return slot


def _compress_sample_kernel(pt_ref, cache_ref, pos_ref, w1k_ref, w1v_ref, w2_ref, o_ref, pbuf, xk_s, xv_s, sem,
                            *, n_pages):
    past = n_pages * PAGE_SIZE
    slot = _pages_pipeline(cache_ref, pt_ref, pbuf, sem, n_pages)
    tail = jnp.zeros((CMP_STRIDE, LANE), F32)
    xk_s[past:past + CMP_STRIDE, :] = tail
    xv_s[past:past + CMP_STRIDE, :] = tail

    def to_rows(i, c):
        for j in range(PAGE_UNROLL):
            p = i * PAGE_UNROLL + j
            t = pbuf[slot, p].T
            r0 = pl.multiple_of(p * PAGE_SIZE, PAGE_SIZE)
            xk_s[pl.ds(r0, PAGE_SIZE), :] = t[:, :KV_W]
            xv_s[pl.ds(r0, PAGE_SIZE), :] = t[:, KV_W:]
        return c
    lax.fori_loop(0, n_pages // PAGE_UNROLL, to_rows, 0)
    n_rows = past // CMP_STRIDE

    def step(r, c):
        row0 = pl.multiple_of(r * CMP_ROWS, CMP_ROWS)
        base = row0 * CMP_STRIDE
        out = _compress_rows(lambda j: xk_s[pl.ds(base + j, CMP_ROWS, stride=CMP_STRIDE), :],
                             lambda j: xv_s[pl.ds(base + j, CMP_ROWS, stride=CMP_STRIDE), :],
                             pos_ref, w1k_ref, w1v_ref, w2_ref)
        idx = row0 + lax.broadcasted_iota(jnp.int32, out.shape, 0)
        o_ref[0, pl.ds(row0, CMP_ROWS), :] = jnp.where(idx < n_rows - 1, out, 0.0).astype(o_ref.dtype)
        return c
    lax.fori_loop(0, n_rows // CMP_ROWS, step, 0)


def _compress_sample(cache, page_table, cw):
    pos2, w1k, w1v, w2 = cw
    nb, n_pages = page_table.shape
    past = n_pages * PAGE_SIZE
    const = lambda shape: pl.BlockSpec(shape, lambda b, pt: (0,) * len(shape), pipeline_mode=pl.Buffered(1))
    return pl.pallas_call(
        functools.partial(_compress_sample_kernel, n_pages=n_pages),
        grid_spec=pltpu.PrefetchScalarGridSpec(
            num_scalar_prefetch=1,
            grid=(nb,),
            in_specs=[pl.BlockSpec(memory_space=pl.ANY),
                      const(pos2.shape), const(w1k.shape), const(w1v.shape), const(w2.shape)],
            out_specs=pl.BlockSpec((1, past // CMP_STRIDE, 2 * KV_W), lambda b, pt: (b, 0, 0)),
            scratch_shapes=[pltpu.VMEM((2, n_pages, PAGE_ROWS, PAGE_SIZE), F32),
                            pltpu.VMEM((past + CMP_STRIDE, LANE), F32), pltpu.VMEM((past + CMP_STRIDE, LANE), F32),
                            pltpu.SemaphoreType.DMA((2,))]),
        out_shape=jax.ShapeDtypeStruct((nb, past // CMP_STRIDE, 2 * KV_W), BF16),
        compiler_params=_cparams("arbitrary"),
        name="compress_sample",
    )(page_table, cache, pos2, w1k, w1v, w2)


ATT_TQ = LANE
ATT_TK = 512
WIN_ROWS = WINDOW + ATT_TQ


def _top_rows(w, idx, n_top):
    alive = idx
    for _ in range(n_top):
        m = jnp.max(w, axis=0, keepdims=True)
        pick = jnp.min(jnp.where(w == m, alive, BIG_I), axis=0, keepdims=True)
        hit = idx == pick
        w = jnp.where(hit, -jnp.inf, w)
        alive = jnp.where(hit, BIG_I, alive)
    return alive


def _select_blocks(imp, sh_s, sel_s, lane_t):
    ncp4 = imp.shape[0] // RATIO
    nsp, nq = sel_s.shape
    imp3 = imp[3 * ncp4:4 * ncp4]
    sh_s[0:SUBLANE, :] = jnp.zeros((SUBLANE, nq), F32)
    sh_s[SUBLANE:SUBLANE + ncp4, :] = imp3
    pslc = imp[0:ncp4] + imp[ncp4:2 * ncp4] + imp[2 * ncp4:3 * ncp4] + imp3 + sh_s[pl.ds(SUBLANE - 1, ncp4), :]
    if nsp > ncp4:
        pslc = jnp.concatenate([pslc, jnp.zeros((nsp - ncp4, nq), F32)], axis=0)
    blk = lax.broadcasted_iota(jnp.int32, (nsp, nq), 0)
    cur = jnp.right_shift(lane_t, 6)
    forced = (blk == 0) | (blk == cur) | (blk == cur - 1)
    w = jnp.where(forced, jnp.inf, jnp.where(blk <= cur, pslc, -jnp.inf))
    alive = _top_rows(w, blk, SEL_TOP)
    sel_s[...] = jnp.where(alive == BIG_I, jnp.where(blk <= cur, 1.0, 0.0), 0.0)


def _attn_kernel(q_ref, ng_ref, kc_ref, kct_ref, cpos_ref, ks_ref, kst_ref, kw_ref, kwt_ref, o_ref,
                 m_s, l_s, acc_s, sel_s, sh_s, gt_s, bias_s, *, q_base, win_base):
    tq, tk = ATT_TQ, ATT_TK
    g = pl.program_id(1)
    t0 = q_base + pl.program_id(2) * tq
    t0f = t0.astype(F32)
    ncp = kc_ref.shape[2]
    slopes = [jnp.where(g == 0, 2.0 ** -(h + 1), 2.0 ** -(GROUP + h + 1)).astype(F32) for h in range(GROUP)]
    hs = [slice(h * tq, (h + 1) * tq) for h in range(GROUP)]
    q = q_ref[0]
    qt = jnp.concatenate([q[:, h * LANE:(h + 1) * LANE].astype(F32).T for h in range(GROUP)], axis=1).astype(BF16)
    lane_t = t0 + lax.broadcasted_iota(jnp.int32, (1, tq), 1)

    sc = _dot(kc_ref[0, 0], qt)
    cpos = cpos_ref[...]
    valid_c = cpos <= lane_t.astype(F32)
    dist = cpos - t0f
    imp = jnp.zeros((ncp, tq), F32)
    pn_parts = []
    for h in range(GROUP):
        s = jnp.where(valid_c, sc[:, hs[h]] + slopes[h] * dist, NEG)
        m = jnp.max(s, axis=0, keepdims=True)
        p = jnp.where(valid_c, jnp.exp(s - m), 0.0)
        pn = p * (1.0 / jnp.maximum(jnp.sum(p, axis=0, keepdims=True), 1e-30))
        imp = imp + pn
        pn_parts.append(pn.astype(BF16))
    oc = _dot(kct_ref[0, 0], jnp.concatenate(pn_parts, axis=1))

    _select_blocks(imp, sh_s, sel_s, lane_t)

    m_s[...] = jnp.full(m_s.shape, NEG, F32)
    l_s[...] = jnp.zeros(l_s.shape, F32)
    acc_s[...] = jnp.zeros(acc_s.shape, F32)
    rowf = lax.broadcasted_iota(jnp.int32, (tk, tq), 0).astype(F32)
    for h in range(GROUP):
        bias_s[h] = slopes[h] * rowf
    blocks = tk // SEL_BLOCK

    def key_tile(kt, causal):
        k0 = pl.multiple_of(kt * tk, tk)
        k_tile = ks_ref[0, 0, pl.ds(k0, tk), :]
        kt_tile = kst_ref[0, 0, :, pl.ds(k0, tk)]
        selm = jnp.concatenate(
            [jnp.broadcast_to(sel_s[pl.ds(kt * blocks + j, 1), :], (SEL_BLOCK, tq)) for j in range(blocks)], axis=0)
        if causal:
            dtab = lax.broadcasted_iota(jnp.int32, (tk, tq), 0) - lax.broadcasted_iota(jnp.int32, (tk, tq), 1)
            selm = jnp.where(dtab <= t0 - k0, selm, 0.0)
        valid = selm > 0.5
        shift = (k0 - t0).astype(F32)
        st = _dot(k_tile, qt)
        p_parts, a_parts = [], []
        for h in range(GROUP):
            c = slopes[h] * shift
            s = jnp.where(valid, st[:, hs[h]] + bias_s[h], NEG)
            m_old = m_s[:, hs[h]]
            m_new = jnp.maximum(m_old, jnp.max(s, axis=0, keepdims=True) + c)
            a = jnp.exp(m_old - m_new)
            p = jnp.exp(s - (m_new - c))
            l_s[:, hs[h]] = a * l_s[:, hs[h]] + jnp.sum(p, axis=0, keepdims=True)
            m_s[:, hs[h]] = m_new
            p_parts.append(p.astype(BF16))
            a_parts.append(a)
        acc_s[...] = acc_s[...] * jnp.concatenate(a_parts, axis=1) + _dot(kt_tile, jnp.concatenate(p_parts, axis=1))

    last = (t0 + tq - 1) // tk
    n_tiles = sel_s.shape[0] // blocks
    assert n_tiles <= 32
    tile_any = jnp.max(jnp.max(sel_s[...].reshape(n_tiles, blocks, tq), axis=1), axis=1, keepdims=True)
    tile_bit = jnp.left_shift(tile_any.astype(jnp.int32), lax.broadcasted_iota(jnp.int32, (n_tiles, 1), 0))
    active = jnp.sum(tile_bit)

    def maybe_tile(kt, carry):
        @pl.when(jnp.bitwise_and(jnp.right_shift(active, kt), 1) == 1)
        def _():
            key_tile(kt, False)
        return carry

    lax.fori_loop(0, last, maybe_tile, 0)
    key_tile(last, True)
    osel = acc_s[...] * (1.0 / l_s[...])

    kr0 = pl.multiple_of(jnp.maximum(t0 - WINDOW - win_base, 0), LANE)
    sw = _dot(kw_ref[0, 0, pl.ds(kr0, WIN_ROWS), :], qt)
    wrow = lax.broadcasted_iota(jnp.int32, (WIN_ROWS, tq), 0)
    dw = (lane_t - win_base - kr0) - wrow
    valid_w = (dw >= 0) & (dw < WINDOW)
    relw = wrow.astype(F32) + (win_base + kr0 - t0).astype(F32)
    pw_parts = []
    for h in range(GROUP):
        s = jnp.where(valid_w, sw[:, hs[h]] + slopes[h] * relw, NEG)
        m = jnp.max(s, axis=0, keepdims=True)
        p = jnp.where(valid_w, jnp.exp(s - m), 0.0)
        pn = p * (1.0 / jnp.maximum(jnp.sum(p, axis=0, keepdims=True), 1e-30))
        pw_parts.append(pn.astype(BF16))
    ow = _dot(kwt_ref[0, 0, :, pl.ds(kr0, WIN_ROWS)], jnp.concatenate(pw_parts, axis=1))

    gt_s[...] = ng_ref[0].T
    o_parts = []
    for h in range(GROUP):
        grow = 3 * (GROUP * g + h)
        o_parts.append(gt_s[pl.ds(grow, 1), :] * oc[HEAD_DIM:, hs[h]]
                       + gt_s[pl.ds(grow + 1, 1), :] * osel[HEAD_DIM:, hs[h]]
                       + gt_s[pl.ds(grow + 2, 1), :] * ow[HEAD_DIM:, hs[h]])
    o_ref[0] = jnp.concatenate(o_parts, axis=0).T.astype(o_ref.dtype)


def _cmp_positions(n_cmp_rows, n_valid):
    p = np.arange(n_cmp_rows)
    c = RATIO * (p % (n_cmp_rows // RATIO)) + p // (n_cmp_rows // RATIO)
    pos = np.where(c < n_valid, c * CMP_STRIDE + CMP_BLOCK - 1, 1e9).astype(np.float32)
    return jnp.asarray(np.broadcast_to(pos[:, None], (n_cmp_rows, LANE)))


def _ratio_major(kc):
    nb, ncp, c = kc.shape
    return kc.reshape(nb, ncp // RATIO, RATIO, c).transpose(0, 2, 1, 3).reshape(nb, ncp, c)


def _attention(q, ng, kc, ks, kw, *, n_cmp_valid, q_base, win_base):
    nb, lq, _ = q.shape
    ncp, tkp, wr = kc.shape[1], ks.shape[1], kw.shape[1]
    nsp = tkp // SEL_BLOCK
    heads = lambda x: x.reshape(x.shape[0], x.shape[1], N_KV, LANE).transpose(0, 2, 1, 3)
    kc, ks, kw = heads(_ratio_major(kc)), heads(ks), heads(kw)
    tr = lambda x: jnp.swapaxes(x, 2, 3)
    cpos = _cmp_positions(ncp, n_cmp_valid)
    r = GROUP * ATT_TQ
    res = lambda rows, cols: pl.BlockSpec((1, 1, rows, cols), lambda b, g, i: (b, g, 0, 0), pipeline_mode=pl.Buffered(1))
    return pl.pallas_call(
        functools.partial(_attn_kernel, q_base=q_base, win_base=win_base),
        grid=(nb, N_KV, lq // ATT_TQ),
        in_specs=[pl.BlockSpec((1, ATT_TQ, GROUP * LANE), lambda b, g, i: (b, i, g)),
                  pl.BlockSpec((1, ATT_TQ, LANE), lambda b, g, i: (b, i, 0)),
                  res(ncp, LANE), res(LANE, ncp),
                  pl.BlockSpec((ncp, LANE), lambda b, g, i: (0, 0), pipeline_mode=pl.Buffered(1)),
                  res(tkp, LANE), res(LANE, tkp), res(wr, LANE), res(LANE, wr)],
        out_specs=pl.BlockSpec((1, ATT_TQ, GROUP * HEAD_DIM), lambda b, g, i: (b, i, g)),
        out_shape=jax.ShapeDtypeStruct((nb, lq, Q_W), BF16),
        scratch_shapes=[pltpu.VMEM((1, r), F32), pltpu.VMEM((1, r), F32), pltpu.VMEM((LANE, r), F32),
                        pltpu.VMEM((nsp, ATT_TQ), F32), pltpu.VMEM((ncp // RATIO + SUBLANE, ATT_TQ), F32),
                        pltpu.VMEM((LANE, ATT_TQ), F32), pltpu.VMEM((GROUP, ATT_TK, ATT_TQ), F32)],
        compiler_params=_cparams("arbitrary", "arbitrary", "arbitrary"),
        name="nsa_attention",
    )(q, ng, kc, tr(kc), cpos, ks, tr(ks), kw, tr(kw))


def _sample_attn_kernel(pt_ref, cache_ref, q2_ref, q4_ref, gate_ref, lt_ref, slope_ref, gsum_ref, kc_ref, kct_ref,
                        cpos_ref, win_ref, ksn_ref, vsn_ref, kwn_ref, vwn_ref, o_ref,
                        pbuf, m_s, l_s, acc_s, sel_s, sh_s, sem, *, n_pages, n_q):
    past = n_pages * PAGE_SIZE
    slot = _pages_pipeline(cache_ref, pt_ref, pbuf, sem, n_pages)
    lane_t = lt_ref[0:1, :]
    tf = lane_t.astype(F32)
    tq = lane_t - past
    slope = slope_ref[0:1, :]
    q2 = q2_ref[0]

    def softmax_update(s_tiles, v_tiles):
        m_old = m_s[...]
        m_new = m_old
        for s in s_tiles:
            m_new = jnp.maximum(m_new, jnp.max(s, axis=0, keepdims=True))
        a = jnp.exp(m_old - m_new)
        l_new = a * l_s[...]
        acc = acc_s[...] * a
        for s, v in zip(s_tiles, v_tiles):
            p = jnp.exp(s - m_new)
            l_new = l_new + jnp.sum(p, axis=0, keepdims=True)
            acc = acc + _dot(v, p.astype(BF16))
        l_s[...] = l_new
        m_s[...] = m_new
        acc_s[...] = acc

    cpos = cpos_ref[...]
    valid_c = cpos <= tf
    s = jnp.where(valid_c, _dot(kc_ref[0], q4_ref[0]) + slope * (cpos - tf), NEG)
    m = jnp.max(s, axis=0, keepdims=True)
    p = jnp.where(valid_c, jnp.exp(s - m), 0.0)
    pn = p * (1.0 / jnp.maximum(jnp.sum(p, axis=0, keepdims=True), 1e-30))
    oc = _dot(kct_ref[0], pn.astype(BF16))
    imp = jnp.dot(pn, gsum_ref[...], preferred_element_type=F32, precision=lax.Precision.HIGHEST)
    _select_blocks(imp, sh_s, sel_s, lane_t)

    m_s[...] = jnp.full(m_s.shape, NEG, F32)
    l_s[...] = jnp.zeros(l_s.shape, F32)
    acc_s[...] = jnp.zeros(acc_s.shape, F32)
    rowi = lax.broadcasted_iota(jnp.int32, (PAGE_SIZE, LANE), 0)
    rowf = rowi.astype(F32)
    blocks = PAGE_SIZE // SEL_BLOCK

    def page_group(i, carry):
        p0 = i * PAGE_UNROLL

        @pl.when(jnp.max(sel_s[pl.ds(p0 * blocks, PAGE_UNROLL * blocks), :]) > 0.5)
        def _():
            s_tiles, v_tiles = [], []
            for j in range(PAGE_UNROLL):
                p = p0 + j
                st = lax.dot_general(pbuf[slot, p, 0:KV_W, :].astype(BF16), q2, TN_DIMS, preferred_element_type=F32)
                selm = jnp.concatenate(
                    [jnp.broadcast_to(sel_s[pl.ds(p * blocks + k, 1), :], (SEL_BLOCK, LANE)) for k in range(blocks)],
                    axis=0)
                pos = rowf + lax.convert_element_type(p * PAGE_SIZE, F32)
                s_tiles.append(jnp.where(selm > 0.5, st + slope * (pos - tf), NEG))
                v_tiles.append(pbuf[slot, p, KV_W:PAGE_ROWS, :].astype(BF16))
            softmax_update(s_tiles, v_tiles)
        return carry

    lax.fori_loop(0, n_pages // PAGE_UNROLL, page_group, 0)
    new_ok = rowi <= tq
    cur_sel = sel_s[pl.ds(n_pages * blocks, 1), :] > 0.5
    s = jnp.where(new_ok, jnp.where(cur_sel, _dot(ksn_ref[0], q2) + slope * (rowf + (past - tf)), NEG), NEG)
    softmax_update([s], [vsn_ref[0]])
    osel = acc_s[...] * (1.0 / jnp.maximum(l_s[...], 1e-30))

    nbuf = win_ref.shape[2]
    wrow = lax.broadcasted_iota(jnp.int32, (nbuf, LANE), 0)
    sw = lax.dot_general(win_ref[0, 0:KV_W, :].astype(BF16), q2, TN_DIMS, preferred_element_type=F32)
    dw = (tq + nbuf) - wrow
    s_old = jnp.where(dw < WINDOW, sw - slope * dw.astype(F32), NEG)
    s_new = jnp.where(new_ok, _dot(kwn_ref[0], q2) + slope * (rowf + (past - tf)), NEG)
    m = jnp.maximum(jnp.max(s_old, axis=0, keepdims=True), jnp.max(s_new, axis=0, keepdims=True))
    p_old = jnp.exp(s_old - m)
    p_new = jnp.exp(s_new - m)
    inv = 1.0 / (jnp.sum(p_old, axis=0, keepdims=True) + jnp.sum(p_new, axis=0, keepdims=True))
    ow = (_dot(win_ref[0, KV_W:PAGE_ROWS, :].astype(BF16), (p_old * inv).astype(BF16))
          + _dot(vwn_ref[0], (p_new * inv).astype(BF16)))

    head0 = lax.broadcasted_iota(jnp.int32, (HEAD_DIM, LANE), 1) < GROUP * n_q
    pick = lambda x, off: jnp.where(head0, x[off:off + HEAD_DIM], x[off + x.shape[0] // 2:off + x.shape[0] // 2 + HEAD_DIM])
    o_ref[0] = (gate_ref[0, 0:1, :] * pick(oc, HEAD_DIM) + gate_ref[0, 1:2, :] * pick(osel, 0)
                + gate_ref[0, 2:3, :] * pick(ow, 0))


def _sample_attention(q, ng, kv_new, kc, cache_sel, win_state, page_table):
    nb, lq, _ = q.shape
    n_pages = page_table.shape[1]
    past = n_pages * PAGE_SIZE
    ncp = kc.shape[1]
    nbuf = win_state.shape[2]
    lanes = N_KV * GROUP * lq
    assert lanes <= LANE // 2 and nbuf == WINDOW and lq <= SUBLANE
    nsp = -(-(past // SEL_BLOCK + 1) // SUBLANE) * SUBLANE

    qh = q.reshape(nb, lq, N_KV, GROUP, 2, HEAD_DIM)[..., 0, :].astype(F32).transpose(0, 2, 3, 1, 4)
    eye = jnp.eye(N_KV, dtype=F32)
    lane_pad = lambda x: jnp.pad(x.reshape(nb, -1, lanes), ((0, 0), (0, 0), (0, LANE - lanes))).astype(BF16)
    q2 = lane_pad(jnp.einsum('bghtd,gk->bkdght', qh, eye))
    q4 = lane_pad(jnp.einsum('bghtd,gk,s->bksdght', qh, eye, jnp.array([1.0, 0.0], F32)))
    gates = ng[:, :, :3 * N_HEADS].reshape(nb, lq, N_KV, GROUP, 3).transpose(0, 4, 2, 3, 1).reshape(nb, 3, lanes)
    gates = jnp.pad(gates, ((0, 0), (0, SUBLANE - 3), (0, LANE - lanes)))
    lane = np.arange(LANE)
    real = lane < lanes
    lane_g, lane_h, lane_q = lane // (GROUP * lq), (lane // lq) % GROUP, lane % lq
    lt = jnp.asarray(np.broadcast_to(np.where(real, past + lane_q, past).astype(np.int32), (SUBLANE, LANE)))
    slope = jnp.asarray(np.broadcast_to(np.where(real, 2.0 ** -(GROUP * lane_g + lane_h + 1.0), 1.0)
                                        .astype(np.float32), (SUBLANE, LANE)))
    same = (lane_g[:, None] == lane_g[None, :]) & (lane_q[:, None] == lane_q[None, :]) & real[:, None] & real[None, :]
    gsum = jnp.asarray(same.astype(np.float32))
    kcp = _ratio_major(kc)
    cpos = _cmp_positions(ncp, past // CMP_STRIDE - 1)
    new_rows = lambda c: jnp.pad(kv_new[:, :, c * KV_W:(c + 1) * KV_W], ((0, 0), (0, LANE - lq), (0, 0))).astype(BF16)
    ksn, vsn, kwn, vwn = new_rows(2), jnp.swapaxes(new_rows(3), 1, 2), new_rows(4), jnp.swapaxes(new_rows(5), 1, 2)

    per_b = lambda *shape: pl.BlockSpec((1,) + shape, lambda b, pt: (b,) + (0,) * len(shape))
    const = lambda arr: pl.BlockSpec(arr.shape, lambda b, pt: (0,) * arr.ndim, pipeline_mode=pl.Buffered(1))
    o = pl.pallas_call(
        functools.partial(_sample_attn_kernel, n_pages=n_pages, n_q=lq),
        grid_spec=pltpu.PrefetchScalarGridSpec(
            num_scalar_prefetch=1,
            grid=(nb,),
            in_specs=[pl.BlockSpec(memory_space=pl.ANY),
                      per_b(KV_W, LANE), per_b(PAGE_ROWS, LANE), per_b(SUBLANE, LANE),
                      const(lt), const(slope), const(gsum),
                      per_b(ncp, PAGE_ROWS), per_b(PAGE_ROWS, ncp), const(cpos),
                      per_b(PAGE_ROWS, nbuf), per_b(LANE, KV_W), per_b(KV_W, LANE), per_b(LANE, KV_W), per_b(KV_W, LANE)],
            out_specs=per_b(HEAD_DIM, LANE),
            scratch_shapes=[pltpu.VMEM((2, n_pages, PAGE_ROWS, PAGE_SIZE), F32),
                            pltpu.VMEM((1, LANE), F32), pltpu.VMEM((1, LANE), F32), pltpu.VMEM((KV_W, LANE), F32),
                            pltpu.VMEM((nsp, LANE), F32), pltpu.VMEM((ncp // RATIO + SUBLANE, LANE), F32),
                            pltpu.SemaphoreType.DMA((2,))]),
        out_shape=jax.ShapeDtypeStruct((nb, HEAD_DIM, LANE), F32),
        compiler_params=_cparams("arbitrary"),
        name="sample_attention",
    )(page_table, cache_sel, q2, q4, gates, lt, slope, gsum, kcp, jnp.swapaxes(kcp, 1, 2), cpos,
      win_state, ksn, vsn, kwn, vwn)
    o = o[:, :, :lanes].reshape(nb, HEAD_DIM, N_KV, GROUP, lq).transpose(0, 4, 2, 3, 1)
    return o.reshape(nb, lq, Q_W).astype(BF16)


HIST = 32


def _mix_kernel(x_ref, a_ref, hist_ref, o_ref, ga_ref, gb_ref, g1_ref, dw_ref, dwb_ref, cg_ref, cb_ref,
                wco_ref, wno_ref, wout_ref, l1g_ref, l1b_ref, x1_ref, ext_s, *, zero_first_hist):
    sb, lb, d = x_ref.shape
    tm = sb * lb
    hist = hist_ref[...]
    if zero_first_hist:
        hist = jnp.where(pl.program_id(1) > 0, hist, 0.0)
    ext_s[:, 0:HIST, :] = hist
    ext_s[:, HIST:HIST + lb, :] = a_ref[...]
    y = jnp.zeros((sb, lb, D_CONV), F32) + dwb_ref[...]
    for j in range(CONV_W):
        y = y + dw_ref[j:j + 1, :] * ext_s[:, pl.ds(j + HIST - (CONV_W - 1), lb), :]
    yn = _ln(y) * cg_ref[...] + cb_ref[...]
    act = (yn * _sigmoid(yn)).reshape(tm, D_CONV)
    out_a = _dot(act.astype(BF16), wco_ref[...])
    out_b = _dot(o_ref[...].reshape(tm, Q_W), wno_ref[...])
    merged = (ga_ref[...].reshape(tm, d).astype(F32) * out_a + gb_ref[...].reshape(tm, d).astype(F32) * out_b)
    mix = _dot(merged.astype(BF16), wout_ref[...]).reshape(sb, lb, d)
    x1_ref[...] = _ln(ALPHA * x_ref[...] + g1_ref[...] * mix) * l1g_ref[...] + l1b_ref[...]


def _mix(x, a, hist, o, ga, gb, mod, lw, tm, *, hist_from_a):
    nseq, seqlen, d = x.shape
    sb, lb, grid = _token_blocks(nseq, seqlen, tm)
    tok = lambda width: pl.BlockSpec((sb, lb, width), lambda s, t: (s, t, 0))
    if hist_from_a:
        hist_spec = pl.BlockSpec((sb, HIST, D_CONV), lambda s, t: (s, jnp.maximum(t * (lb // HIST) - 1, 0), 0))
    else:
        hist_spec = pl.BlockSpec((sb, HIST, D_CONV), lambda s, t: (s, 0, 0))
    const = lambda arr: pl.BlockSpec(arr.shape, lambda s, t: (0,) * arr.ndim, pipeline_mode=pl.Buffered(1))
    return pl.pallas_call(
        functools.partial(_mix_kernel, zero_first_hist=hist_from_a),
        grid=grid,
        in_specs=[tok(d), tok(D_CONV), hist_spec, tok(Q_W), tok(d), tok(d),
                  pl.BlockSpec((sb, 1, d), lambda s, t: (s, 0, 2))] + [const(w) for w in lw],
        out_specs=tok(d),
        out_shape=jax.ShapeDtypeStruct((nseq, seqlen, d), F32),
        scratch_shapes=[pltpu.VMEM((sb, HIST + lb, D_CONV), F32)],
        compiler_params=_cparams("parallel", "arbitrary"),
        name="mix",
    )(x, a, hist, o, ga, gb, mod, *lw)


NO_RANK = 99
CAND_ROWS = PEER_TOPK + SUBLANE * (PEER_TOPK - 1)


def _top_ranked(w, idx, n_top, tie_order):
    rank = jnp.full(w.shape, NO_RANK, jnp.int32)
    vals = []
    for k in range(n_top):
        m = jnp.max(w, axis=0, keepdims=True)
        hit = w == m
        if tie_order:
            hit = idx == jnp.min(jnp.where(hit, idx, BIG_I), axis=0, keepdims=True)
        w = jnp.where(hit, -jnp.inf, w)
        rank = jnp.where(hit, k, rank)
        vals.append(m)
    return rank, vals


def _peer_sel_kernel(x1_ref, sh_ref, sc_ref, wq_ref, keys_ref, h2t_ref, n0_ref, e0_ref, r1_ref, e1_ref, q_s):
    sb, lb, d = x1_ref.shape
    tm = sb * lb
    h2 = (_ln(x1_ref[...]) * (1.0 + sc_ref[...]) + sh_ref[...]).reshape(tm, d)
    h2t_ref[...] = h2.T.astype(BF16)
    q_s[...] = _dot(h2.astype(BF16), wq_ref[...]).astype(BF16)
    idx = lax.broadcasted_iota(jnp.int32, (N_KEYS, tm), 0)
    row8 = lax.broadcasted_iota(jnp.int32, (SUBLANE, tm), 0)
    cidx = lax.broadcasted_iota(jnp.int32, (CAND_ROWS, tm), 0)

    def retrieve(h, s0, s1, tie_order):
        rank0, v0 = _top_ranked(s0, idx, PEER_TOPK, tie_order)
        rank1, v1 = _top_ranked(s1, idx, PEER_TOPK, tie_order)
        v1a = jnp.concatenate(v1[:SUBLANE], axis=0)
        v1b = jnp.concatenate(v1[SUBLANE:], axis=0)
        pieces = [v1a + v0[0], v1b + v0[0]]
        for i in range(1, PEER_TOPK):
            pieces.append(jnp.where(row8 < PEER_TOPK // (i + 1), v1a + v0[i], -jnp.inf))
        cand = jnp.concatenate(pieces, axis=0)
        crank, _ = _top_ranked(cand, cidx, PEER_TOPK, tie_order)
        chosen = crank < PEER_TOPK
        top = v0[0] + v1[0]
        z = jnp.sum(jnp.where(chosen, jnp.exp(cand - top), 0.0), axis=0, keepdims=True)
        cf = jnp.where(chosen, 1.0, 0.0)
        counts = [jnp.sum(cf[0:2 * SUBLANE], axis=0, keepdims=True)]
        for i in range(1, PEER_TOPK):
            counts.append(jnp.sum(cf[SUBLANE * (i + 1):SUBLANE * (i + 2)], axis=0, keepdims=True))
        n0 = jnp.zeros((N_KEYS, tm), F32)
        for i in range(PEER_TOPK):
            n0 = jnp.where(rank0 == i, counts[i], n0)
        n0_ref[h] = n0
        e0_ref[h] = jnp.where(rank0 < PEER_TOPK, jnp.exp(s0 - v0[0]), 0.0)
        r1_ref[h] = rank1.astype(F32).astype(BF16)
        e1_ref[h] = (jnp.where(rank1 < PEER_TOPK, jnp.exp(s1 - v1[0]), 0.0) * (1.0 / z)).astype(BF16)
        ranked = lambda r: jnp.sum(jnp.where(r < PEER_TOPK, 1.0, 0.0), axis=0, keepdims=True)
        return ranked(rank0) + ranked(rank1) + ranked(crank)

    def head(h, carry):
        s0 = lax.dot_general(keys_ref[2 * h], q_s[:, pl.ds(pl.multiple_of(2 * h * N_KEYS, N_KEYS), N_KEYS)], NT_DIMS,
                             preferred_element_type=F32)
        s1 = lax.dot_general(keys_ref[2 * h + 1], q_s[:, pl.ds(pl.multiple_of((2 * h + 1) * N_KEYS, N_KEYS), N_KEYS)],
                             NT_DIMS, preferred_element_type=F32)
        n_ranked = retrieve(h, s0, s1, False)

        @pl.when(jnp.max(n_ranked) > 3.0 * PEER_TOPK)
        def _():
            retrieve(h, s0, s1, True)
        return carry

    lax.fori_loop(0, PEER_HEADS, head, 0)


def _peer_sel(x1, mod, wq, keys, tm):
    nseq, seqlen, d = x1.shape
    n = nseq * seqlen
    sb, lb, grid = _token_blocks(nseq, seqlen, tm)
    tm = sb * lb
    tok = pl.BlockSpec((sb, lb, d), lambda s, t: (s, t, 0))
    flat = lambda s, t: s * grid[1] + t
    stat = pl.BlockSpec((PEER_HEADS, N_KEYS, tm), lambda s, t: (0, 0, flat(s, t)))
    const = lambda arr: pl.BlockSpec(arr.shape, lambda s, t: (0,) * arr.ndim, pipeline_mode=pl.Buffered(1))
    stat_shape = jax.ShapeDtypeStruct((PEER_HEADS, N_KEYS, n), F32)
    slab_shape = jax.ShapeDtypeStruct((PEER_HEADS, N_KEYS, n), BF16)
    return pl.pallas_call(
        _peer_sel_kernel,
        grid=grid,
        in_specs=[tok, pl.BlockSpec((sb, 1, d), lambda s, t: (s, 0, 3)), pl.BlockSpec((sb, 1, d), lambda s, t: (s, 0, 4)),
                  const(wq), const(keys)],
        out_specs=[pl.BlockSpec((d, tm), lambda s, t: (0, flat(s, t))), stat, stat, stat, stat],
        out_shape=[jax.ShapeDtypeStruct((d, n), BF16), stat_shape, stat_shape, slab_shape, slab_shape],
        scratch_shapes=[pltpu.VMEM((tm, 2 * PEER_HEADS * N_KEYS), BF16)],
        compiler_params=_cparams("parallel", "parallel"),
        name="peer_select",
    )(x1, mod, mod, wq, keys)


PEER_CE = 1024


def _peer_ffn_kernel(h2t_ref, u_ref, vt_ref, n0_ref, e0_ref, r1_ref, e1_ref, x1_ref, g2_ref, l2g_ref, l2b_ref,
                     y_ref, acc_s, wg_s):
    c = pl.program_id(2)

    @pl.when(c == 0)
    def _():
        acc_s[...] = jnp.zeros(acc_s.shape, F32)

    ut = _dot(u_ref[...], h2t_ref[...])
    for e in range(PEER_CE // N_KEYS):
        rows = slice(e * N_KEYS, (e + 1) * N_KEYS)
        w = jnp.zeros((N_KEYS, ut.shape[1]), BF16)
        for h in range(PEER_HEADS):
            slab = lambda ref: jnp.tile(jnp.broadcast_to(ref[h, e:e + 1, :], (2 * SUBLANE, ut.shape[1])).astype(BF16),
                                        (N_KEYS // (2 * SUBLANE), 1))
            w = w + jnp.where(r1_ref[h] < slab(n0_ref), e1_ref[h], jnp.zeros((), BF16)) * slab(e0_ref)
        wg_s[rows, :] = w * _gelu(ut[rows]).astype(BF16)
    acc_s[...] += _dot(vt_ref[...], wg_s[...])

    @pl.when(c == pl.num_programs(2) - 1)
    def _():
        sb, lb, d = x1_ref.shape
        f = acc_s[...].T.reshape(sb, lb, d)
        y_ref[...] = _ln(ALPHA * x1_ref[...] + g2_ref[...] * f) * l2g_ref[...] + l2b_ref[...]


def _peer_ffn(h2t, u, vt, stats, x1, mod, l2g, l2b, tm):
    nseq, seqlen, d = x1.shape
    sb, lb, grid = _token_blocks(nseq, seqlen, tm)
    tm = sb * lb
    n_exp = u.shape[0]
    flat = lambda s, t: s * grid[1] + t
    tok = pl.BlockSpec((sb, lb, d), lambda s, t, c: (s, t, 0))
    c1 = PEER_CE // N_KEYS
    row_stat = pl.BlockSpec((PEER_HEADS, c1, tm), lambda s, t, c: (0, c, flat(s, t)))
    slab_stat = pl.BlockSpec((PEER_HEADS, N_KEYS, tm), lambda s, t, c: (0, 0, flat(s, t)))
    vec = pl.BlockSpec((1, d), lambda s, t, c: (0, 0))
    n0, e0, r1, e1 = stats
    return pl.pallas_call(
        _peer_ffn_kernel,
        grid=grid + (n_exp // PEER_CE,),
        in_specs=[pl.BlockSpec((d, tm), lambda s, t, c: (0, flat(s, t))),
                  pl.BlockSpec((PEER_CE, d), lambda s, t, c: (c, 0)),
                  pl.BlockSpec((d, PEER_CE), lambda s, t, c: (0, c)),
                  row_stat, row_stat, slab_stat, slab_stat, tok,
                  pl.BlockSpec((sb, 1, d), lambda s, t, c: (s, 0, 5)), vec, vec],
        out_specs=tok,
        out_shape=jax.ShapeDtypeStruct((nseq, seqlen, d), F32),
        scratch_shapes=[pltpu.VMEM((d, tm), F32), pltpu.VMEM((PEER_CE, tm), BF16)],
        compiler_params=_cparams("parallel", "parallel", "arbitrary"),
        name="peer_experts",
    )(h2t, u, vt, n0, e0, r1, e1, x1, mod, l2g, l2b)


TOKEN_TILE = 512
PEER_SEL_TILE = 256


def _channel_mixing(x, mod, a, o, ga, gb, hist, lw_mix, peer, *, hist_from_a):
    x1 = _mix(x, a, hist, o, ga, gb, mod, lw_mix, TOKEN_TILE, hist_from_a=hist_from_a)
    wq, keys, u, vt, l2g, l2b = peer
    h2t, *stats = _peer_sel(x1, mod, wq, keys, PEER_SEL_TILE)
    return _peer_ffn(h2t, u, vt, stats, x1, mod, l2g, l2b, TOKEN_TILE)


def kernel(x_prompt, x_sample, cache_cmp_kv, cache_sel_kv, state_win_kv, state_conv, page_table, c_prompt, c_sample,
           w_ada, b_ada, w_in, b_in, conv_dw, conv_dw_b, conv_ln_g, conv_ln_b, w_conv_out, cmp_pos, w_cmp_k1, w_cmp_k2,
           w_cmp_v1, w_cmp_v2, w_nsa_out, w_out, ln1_g, ln1_b, peer_wq, peer_keys, peer_u, peer_v, ln2_g, ln2_b):
    assert w_ada.shape[0] == DEPTH == 1
    l = 0
    nbp, s, d = x_prompt.shape
    nbs, ls, _ = x_sample.shape
    n_pages = page_table.shape[1]
    past = n_pages * PAGE_SIZE
    nbuf = state_win_kv.shape[2]
    n_phys = cache_cmp_kv.shape[1]
    assert s % (CMP_ROWS * CMP_STRIDE) == 0 and past % (CMP_ROWS * CMP_STRIDE) == 0 and nbuf == WINDOW
    assert n_pages % PAGE_UNROLL == 0

    w_ext, b_ext = _extend_w_in(w_in[l], b_in[l])
    cw = _compress_weights(cmp_pos[l], w_cmp_k1[l], w_cmp_k2[l], w_cmp_v1[l], w_cmp_v2[l])
    row = lambda v: v.reshape(1, -1).astype(F32)
    lw_mix = (jnp.pad(conv_dw[l], ((0, HIST - CONV_W), (0, 0))), row(conv_dw_b[l]), row(conv_ln_g[l]), row(conv_ln_b[l]),
              w_conv_out[l].astype(BF16), w_nsa_out[l].astype(BF16), w_out[l].astype(BF16), row(ln1_g[l]), row(ln1_b[l]))
    peer = (peer_wq[l].astype(BF16), peer_keys[l].reshape(2 * PEER_HEADS, N_KEYS, -1).astype(BF16),
            peer_u[l].astype(BF16), peer_v[l].T.astype(BF16), row(ln2_g[l]), row(ln2_b[l]))

    c_all = jnp.concatenate([c_prompt, c_sample], axis=0)
    c_all = jnp.pad(c_all, ((0, (-c_all.shape[0]) % SUBLANE), (0, 0)))
    mod = _adaln(c_all, w_ada[l], b_ada[l])
    mod_p = mod[:nbp].reshape(nbp, 1, -1)
    mod_s = mod[nbp:nbp + nbs].reshape(nbs, 1, -1)

    a_p, q_p, kv_p, kvg_p, ng_p, ga_p, gb_p = _inproj(x_prompt, mod_p, w_ext, b_ext, TOKEN_TILE)
    kc_p = _compress_prompt(kv_p[:, :, :2 * KV_W].reshape(nbp, 2 * s, LANE), cw)
    o_p = _attention(q_p, ng_p, kc_p, kvg_p[:, :, 2 * KV_W:4 * KV_W], kvg_p[:, :, 4 * KV_W:],
                     n_cmp_valid=s // CMP_STRIDE - 1, q_base=0, win_base=0)
    y_p = _channel_mixing(x_prompt, mod_p, a_p, o_p, ga_p, gb_p, a_p, lw_mix, peer, hist_from_a=True)

    a_s, q_s, kv_s, kvg_s, ng_s, ga_s, gb_s = _inproj(x_sample, mod_s, w_ext, b_ext, TOKEN_TILE)
    pages = lambda cache: cache[l].transpose(0, 2, 3, 4, 1).reshape(n_phys, PAGE_ROWS, PAGE_SIZE)
    kc_s = _compress_sample(pages(cache_cmp_kv), page_table, cw)
    win_t = state_win_kv[l].transpose(0, 2, 3, 4, 1).reshape(nbs, PAGE_ROWS, nbuf)
    o_s = _sample_attention(q_s, ng_s, kv_s, kc_s, pages(cache_sel_kv), win_t, page_table)
    hist_s = jnp.pad(state_conv[l], ((0, 0), (HIST - (CONV_W - 1), 0), (0, 0)))
    y_s = _channel_mixing(x_sample, mod_s, a_s, o_s, ga_s, gb_s, hist_s, lw_mix, peer, hist_from_a=False)

    kv6 = lambda t, c: t[:, :, c * 2 * KV_W:(c + 1) * 2 * KV_W].reshape(1, t.shape[0], t.shape[1], 2, N_KV, HEAD_DIM)
    nb = min(WINDOW, s)
    win_new = jnp.concatenate([state_win_kv[l], kv6(kv_s, 2)[0]], axis=1)[:, ls:]
    conv_new = jnp.concatenate([state_conv[l], a_s], axis=1)[:, ls:]
    return (y_p, y_s, kv6(kv_p, 0), kv6(kv_p, 1), kv6(kv_p, 2)[:, :, s - nb:], a_p[None, :, s - (CONV_W - 1):],
            kv6(kv_s, 0), kv6(kv_s, 1), win_new[None], conv_new[None])
```

```python
import functools

import numpy as np
import jax
import jax.numpy as jnp
from jax import lax
from jax.experimental import pallas as pl
from jax.experimental.pallas import tpu as pltpu

F32 = jnp.float32
BF16 = jnp.bfloat16

D_MODEL = 1024
D_CONV = 512
CONV_W = 31
N_HEADS = 8
HEAD_DIM = 64
N_KV = 2
GROUP = N_HEADS // N_KV
CMP_BLOCK = 32
CMP_STRIDE = 16
CMP_HID = 256
SEL_BLOCK = 64
SEL_TOP = 16
RATIO = SEL_BLOCK // CMP_STRIDE
WINDOW = 512
PAGE_SIZE = 128
PEER_HEADS = 8
N_KEYS = 128
PEER_TOPK = 16
DEPTH = 1
ALPHA = (2.0 * DEPTH) ** 0.25
LN_EPS = 1e-5
Q_W = N_HEADS * HEAD_DIM
KV_W = N_KV * HEAD_DIM

LANE = 128
SUBLANE = 8
VMEM_LIMIT = 56 * 1024 * 1024
NEG = -1e30
BIG_I = 1 << 20

SEG_A = 2 * D_CONV
SEG_Q = N_HEADS * LANE
SEG_KV = 6 * KV_W
SEG_KVG = 6 * KV_W
SEG_G = LANE
SEG_M = 2 * D_MODEL
OFF_A = 0
OFF_Q = OFF_A + SEG_A
OFF_KV = OFF_Q + SEG_Q
OFF_KVG = OFF_KV + SEG_KV
OFF_G = OFF_KVG + SEG_KVG
OFF_M = OFF_G + SEG_G
W_EXT = OFF_M + SEG_M

TN_DIMS = (((0,), (0,)), ((), ()))
NT_DIMS = (((1,), (1,)), ((), ()))


def _cparams(*sem):
    return pltpu.CompilerParams(dimension_semantics=sem, vmem_limit_bytes=VMEM_LIMIT)


def _ln(x):
    mu = jnp.mean(x, axis=-1, keepdims=True)
    xc = x - mu
    return xc * lax.rsqrt(jnp.mean(xc * xc, axis=-1, keepdims=True) + LN_EPS)


def _sigmoid(x):
    return 1.0 / (1.0 + jnp.exp(-x))


def _gelu(x):
    hx = 0.5 * x
    return hx + hx * jnp.tanh(x * (0.7978845608028654 + (0.7978845608028654 * 0.044715) * (x * x)))


def _dot(a, b):
    return jnp.dot(a, b, preferred_element_type=F32)


def _adaln_kernel(c_ref, w_ref, b_ref, o_ref):
    c = c_ref[...]
    s = c * _sigmoid(c)
    o_ref[...] = _dot(s.astype(BF16), w_ref[...].astype(BF16)) + b_ref[...]


def _adaln(c, w, b):
    rows, d = c.shape
    n = w.shape[1]
    tn = 1024
    return pl.pallas_call(
        _adaln_kernel,
        grid=(n // tn,),
        in_specs=[pl.BlockSpec((rows, d), lambda j: (0, 0)),
                  pl.BlockSpec((d, tn), lambda j: (0, j)),
                  pl.BlockSpec((1, tn), lambda j: (0, j))],
        out_specs=pl.BlockSpec((rows, tn), lambda j: (0, j)),
        out_shape=jax.ShapeDtypeStruct((rows, n), F32),
        compiler_params=_cparams("parallel"),
        name="adaln",
    )(c, w, b.reshape(1, n))


def _inproj_kernel(x_ref, mod_ref, w_ref, b_ref, a_ref, q_ref, kv_ref, kvg_ref, ng_ref, ga_ref, gb_ref):
    sb, lb, d = x_ref.shape
    tm = sb * lb
    mod = mod_ref[...]
    h = _ln(x_ref[...]) * (1.0 + mod[:, :, d:2 * d]) + mod[:, :, 0:d]
    hb = h.reshape(tm, d).astype(BF16)

    def seg(off, width):
        return _dot(hb, w_ref[:, off:off + width]) + b_ref[:, off:off + width]

    z = seg(OFF_A, SEG_A)
    a_ref[...] = (z[:, :D_CONV] * _sigmoid(z[:, D_CONV:])).reshape(sb, lb, D_CONV)
    q_ref[...] = seg(OFF_Q, SEG_Q).astype(BF16).reshape(sb, lb, SEG_Q)
    kv_ref[...] = seg(OFF_KV, SEG_KV).reshape(sb, lb, SEG_KV)
    kvg_ref[...] = seg(OFF_KVG, SEG_KVG).astype(BF16).reshape(sb, lb, SEG_KVG)
    ng_ref[...] = _sigmoid(seg(OFF_G, SEG_G)).reshape(sb, lb, SEG_G)
    ga_ref[...] = _sigmoid(seg(OFF_M, D_MODEL)).astype(BF16).reshape(sb, lb, D_MODEL)
    gb_ref[...] = _sigmoid(seg(OFF_M + D_MODEL, D_MODEL)).astype(BF16).reshape(sb, lb, D_MODEL)


def _token_blocks(nseq, seqlen, tm):
    if seqlen >= tm:
        return 1, tm, (nseq, seqlen // tm)
    sb = min(tm // seqlen, nseq)
    return sb, seqlen, (nseq // sb, 1)


def _inproj(x, mod, w_ext, b_ext, tm):
    nseq, seqlen, d = x.shape
    sb, lb, grid = _token_blocks(nseq, seqlen, tm)
    tok = lambda width: pl.BlockSpec((sb, lb, width), lambda s, t: (s, t, 0))
    widths = (D_CONV, SEG_Q, SEG_KV, SEG_KVG, SEG_G, D_MODEL, D_MODEL)
    dtypes = (F32, BF16, F32, BF16, F32, BF16, BF16)
    return pl.pallas_call(
        _inproj_kernel,
        grid=grid,
        in_specs=[tok(d),
                  pl.BlockSpec((sb, 1, 2 * d), lambda s, t: (s, 0, 0)),
                  pl.BlockSpec((d, W_EXT), lambda s, t: (0, 0), pipeline_mode=pl.Buffered(1)),
                  pl.BlockSpec((1, W_EXT), lambda s, t: (0, 0), pipeline_mode=pl.Buffered(1))],
        out_specs=[tok(w) for w in widths],
        out_shape=[jax.ShapeDtypeStruct((nseq, seqlen, w), dt) for w, dt in zip(widths, dtypes)],
        compiler_params=_cparams("parallel", "parallel"),
        name="inproj",
    )(x, mod, w_ext, b_ext)


def _extend_w_in(w_in, b_in):
    scale = HEAD_DIM ** -0.5
    c_q = 2 * D_CONV
    c_kv = c_q + Q_W
    c_g = c_kv + 6 * KV_W
    c_m = c_g + 3 * N_HEADS
    cols_w, cols_b = [], []

    def add(w, b):
        cols_w.append(w)
        cols_b.append(b)

    add(w_in[:, :c_q], b_in[:c_q])
    zw = jnp.zeros((w_in.shape[0], HEAD_DIM), w_in.dtype)
    zb = jnp.zeros((HEAD_DIM,), b_in.dtype)
    for h in range(N_HEADS):
        sl = slice(c_q + h * HEAD_DIM, c_q + (h + 1) * HEAD_DIM)
        add(w_in[:, sl] * scale, b_in[sl] * scale)
        add(zw, zb)
    add(w_in[:, c_kv:c_g], b_in[c_kv:c_g])
    for br in range(3):
        k0 = c_kv + br * 2 * KV_W
        v0 = k0 + KV_W
        for g in range(N_KV):
            ks = slice(k0 + g * HEAD_DIM, k0 + (g + 1) * HEAD_DIM)
            vs = slice(v0 + g * HEAD_DIM, v0 + (g + 1) * HEAD_DIM)
            add(w_in[:, ks], b_in[ks])
            add(w_in[:, vs], b_in[vs])
    add(w_in[:, c_g:c_m], b_in[c_g:c_m])
    add(jnp.zeros((w_in.shape[0], SEG_G - 3 * N_HEADS), w_in.dtype), jnp.zeros((SEG_G - 3 * N_HEADS,), b_in.dtype))
    add(w_in[:, c_m:], b_in[c_m:])
    w = jnp.concatenate(cols_w, axis=1).astype(BF16)
    b = jnp.concatenate(cols_b, axis=0).reshape(1, -1).astype(F32)
    assert w.shape[1] == W_EXT
    return w, b


CMP_ROWS = 256


def _compress_rows(load_k, load_v, pos_ref, w1k_ref, w1v_ref, w2_ref):
    acc_k = jnp.zeros((CMP_ROWS, 2 * CMP_HID), F32)
    acc_v = jnp.zeros((CMP_ROWS, 2 * CMP_HID), F32)
    for j in range(0, CMP_BLOCK, 2):
        pos = [pos_ref[j + i:j + i + 1, :] for i in range(2)]
        xk = jnp.concatenate([(load_k(j + i) + pos[i]).astype(BF16) for i in range(2)], axis=1)
        xv = jnp.concatenate([(load_v(j + i) + pos[i]).astype(BF16) for i in range(2)], axis=1)
        acc_k = acc_k + _dot(xk, w1k_ref[j // 2])
        acc_v = acc_v + _dot(xv, w1v_ref[j // 2])
    hk = _gelu(acc_k)
    hv = _gelu(acc_v)
    outs = []
    for g in range(N_KV):
        hg = jnp.concatenate([hk[:, g * CMP_HID:(g + 1) * CMP_HID], hv[:, g * CMP_HID:(g + 1) * CMP_HID]], axis=1)
        outs.append(_dot(hg.astype(BF16), w2_ref[...]))
    return jnp.concatenate(outs, axis=1)


def _compress_weights(cmp_pos, w_k1, w_k2, w_v1, w_v2):
    def bd1(w1):
        w = w1.reshape(CMP_BLOCK, HEAD_DIM, CMP_HID)
        z = jnp.zeros_like(w)
        top = jnp.concatenate([w, z], axis=2)
        bot = jnp.concatenate([z, w], axis=2)
        wbd = jnp.concatenate([top, bot], axis=1)
        return wbd.reshape(CMP_BLOCK // 2, 2 * KV_W, 2 * CMP_HID).astype(BF16)
    zk = jnp.zeros_like(w_k2)
    w2 = jnp.concatenate([jnp.concatenate([w_k2, zk], axis=1),
                          jnp.concatenate([zk, w_v2], axis=1)], axis=0).astype(BF16)
    pos2 = jnp.tile(cmp_pos, (1, N_KV)).astype(F32)
    return pos2, bd1(w_k1), bd1(w_v1), w2


def _compress_prompt_kernel(x_ref, halo_ref, pos_ref, w1k_ref, w1v_ref, w2_ref, o_ref, xs_ref, *, n_valid):
    rows2 = x_ref.shape[1]
    xs_ref[0:rows2, :] = x_ref[0]
    xs_ref[rows2:rows2 + 2 * CMP_STRIDE, :] = halo_ref[0]
    n_rows = rows2 // (2 * CMP_STRIDE)
    first = pl.program_id(1) * n_rows
    for r in range(n_rows // CMP_ROWS):
        base = 2 * CMP_STRIDE * CMP_ROWS * r
        load = lambda off: (lambda j: xs_ref[pl.ds(base + 2 * j + off, CMP_ROWS, stride=2 * CMP_STRIDE), :])
        out = _compress_rows(load(0), load(1), pos_ref, w1k_ref, w1v_ref, w2_ref)
        idx = first + r * CMP_ROWS + lax.broadcasted_iota(jnp.int32, out.shape, 0)
        o_ref[0, r * CMP_ROWS:(r + 1) * CMP_ROWS, :] = jnp.where(idx < n_valid, out, 0.0).astype(o_ref.dtype)


def _compress_prompt(kvc, cw):
    nb, s2, _ = kvc.shape
    s = s2 // 2
    pos2, w1k, w1v, w2 = cw
    p = min(CMP_ROWS * CMP_STRIDE, s)
    n_ch = s // CMP_STRIDE
    rows = p // CMP_STRIDE
    const = lambda shape: pl.BlockSpec(shape, lambda b, i: (0,) * len(shape), pipeline_mode=pl.Buffered(1))
    return pl.pallas_call(
        functools.partial(_compress_prompt_kernel, n_valid=n_ch - 1),
        grid=(nb, s // p),
        in_specs=[pl.BlockSpec((1, 2 * p, LANE), lambda b, i: (b, i, 0)),
                  pl.BlockSpec((1, 2 * CMP_STRIDE, LANE), lambda b, i: (b, jnp.minimum((i + 1) * rows, n_ch - 1), 0)),
                  const(pos2.shape), const(w1k.shape), const(w1v.shape), const(w2.shape)],
        out_specs=pl.BlockSpec((1, rows, 2 * KV_W), lambda b, i: (b, i, 0)),
        out_shape=jax.ShapeDtypeStruct((nb, n_ch, 2 * KV_W), BF16),
        scratch_shapes=[pltpu.VMEM((2 * p + 2 * CMP_STRIDE, LANE), F32)],
        compiler_params=_cparams("parallel", "parallel"),
        name="compress_prompt",
    )(kvc, kvc, pos2, w1k, w1v, w2)


PAGE_ROWS = 2 * KV_W
PAGE_UNROLL = 4


def _page_copy(cache_ref, pt_ref, buf_ref, sem_ref, b, p, slot):
    return pltpu.make_async_copy(cache_ref.at[pt_ref[b, p]], buf_ref.at[slot, p], sem_ref.at[slot])


def _pages_start(cache_ref, pt_ref, buf_ref, sem_ref, b, slot, n_pages):
    def body(p, c):
        _page_copy(cache_ref, pt_ref, buf_ref, sem_ref, b, p, slot).start()
        return c
    lax.fori_loop(0, n_pages, body, 0)


def _pages_wait(cache_ref, pt_ref, buf_ref, sem_ref, b, slot, n_pages):
    def body(p, c):
        _page_copy(cache_ref, pt_ref, buf_ref, sem_ref, b, p, slot).wait()
        return c
    lax.fori_loop(0, n_pages, body, 0)


def _pages_pipeline(cache_ref, pt_ref, buf_ref, sem_ref, n_pages):
    b = pl.program_id(0)
    slot = lax.rem(b, 2)

    @pl.when(b == 0)
    def _():
        _pages_start(cache_ref, pt_ref, buf_ref, sem_ref, b, slot, n_pages)

    @pl.when(b + 1 < pl.num_programs(0))
    def _():
        _pages_start(cache_ref, pt_ref, buf_ref, sem_ref, b + 1, 1 - slot, n_pages)

    _pages_wait(cache_ref, pt_ref, buf_ref, sem_ref, b, slot, n_pages)
    return slot


def _compress_sample_kernel(pt_ref, cache_ref, pos_ref, w1k_ref, w1v_ref, w2_ref, o_ref, pbuf, xk_s, xv_s, sem,
                            *, n_pages):
    past = n_pages * PAGE_SIZE
    slot = _pages_pipeline(cache_ref, pt_ref, pbuf, sem, n_pages)
    tail = jnp.zeros((CMP_STRIDE, LANE), F32)
    xk_s[past:past + CMP_STRIDE, :] = tail
    xv_s[past:past + CMP_STRIDE, :] = tail

    def to_rows(i, c):
        for j in range(PAGE_UNROLL):
            p = i * PAGE_UNROLL + j
            t = pbuf[slot, p].T
            r0 = pl.multiple_of(p * PAGE_SIZE, PAGE_SIZE)
            xk_s[pl.ds(r0, PAGE_SIZE), :] = t[:, :KV_W]
            xv_s[pl.ds(r0, PAGE_SIZE), :] = t[:, KV_W:]
        return c
    lax.fori_loop(0, n_pages // PAGE_UNROLL, to_rows, 0)
    n_rows = past // CMP_STRIDE

    def step(r, c):
        row0 = pl.multiple_of(r * CMP_ROWS, CMP_ROWS)
        base = row0 * CMP_STRIDE
        out = _compress_rows(lambda j: xk_s[pl.ds(base + j, CMP_ROWS, stride=CMP_STRIDE), :],
                             lambda j: xv_s[pl.ds(base + j, CMP_ROWS, stride=CMP_STRIDE), :],
                             pos_ref, w1k_ref, w1v_ref, w2_ref)
        idx = row0 + lax.broadcasted_iota(jnp.int32, out.shape, 0)
        o_ref[0, pl.ds(row0, CMP_ROWS), :] = jnp.where(idx < n_rows - 1, out, 0.0).astype(o_ref.dtype)
        return c
    lax.fori_loop(0, n_rows // CMP_ROWS, step, 0)


def _compress_sample(cache, page_table, cw):
    pos2, w1k, w1v, w2 = cw
    nb, n_pages = page_table.shape
    past = n_pages * PAGE_SIZE
    const = lambda shape: pl.BlockSpec(shape, lambda b, pt: (0,) * len(shape), pipeline_mode=pl.Buffered(1))
    return pl.pallas_call(
        functools.partial(_compress_sample_kernel, n_pages=n_pages),
        grid_spec=pltpu.PrefetchScalarGridSpec(
            num_scalar_prefetch=1,
            grid=(nb,),
            in_specs=[pl.BlockSpec(memory_space=pl.ANY),
                      const(pos2.shape), const(w1k.shape), const(w1v.shape), const(w2.shape)],
            out_specs=pl.BlockSpec((1, past // CMP_STRIDE, 2 * KV_W), lambda b, pt: (b, 0, 0)),
            scratch_shapes=[pltpu.VMEM((2, n_pages, PAGE_ROWS, PAGE_SIZE), F32),
                            pltpu.VMEM((past + CMP_STRIDE, LANE), F32), pltpu.VMEM((past + CMP_STRIDE, LANE), F32),
                            pltpu.SemaphoreType.DMA((2,))]),
        out_shape=jax.ShapeDtypeStruct((nb, past // CMP_STRIDE, 2 * KV_W), BF16),
        compiler_params=_cparams("arbitrary"),
        name="compress_sample",
    )(page_table, cache, pos2, w1k, w1v, w2)


ATT_TQ = LANE
ATT_TK = 512
WIN_ROWS = WINDOW + ATT_TQ


def _top_rows(w, idx, n_top):
    alive = idx
    for _ in range(n_top):
        m = jnp.max(w, axis=0, keepdims=True)
        pick = jnp.min(jnp.where(w == m, alive, BIG_I), axis=0, keepdims=True)
        hit = idx == pick
        w = jnp.where(hit, -jnp.inf, w)
        alive = jnp.where(hit, BIG_I, alive)
    return alive


def _select_blocks(imp, sh_s, sel_s, lane_t):
    ncp4 = imp.shape[0] // RATIO
    nsp, nq = sel_s.shape
    imp3 = imp[3 * ncp4:4 * ncp4]
    sh_s[0:SUBLANE, :] = jnp.zeros((SUBLANE, nq), F32)
    sh_s[SUBLANE:SUBLANE + ncp4, :] = imp3
    pslc = imp[0:ncp4] + imp[ncp4:2 * ncp4] + imp[2 * ncp4:3 * ncp4] + imp3 + sh_s[pl.ds(SUBLANE - 1, ncp4), :]
    if nsp > ncp4:
        pslc = jnp.concatenate([pslc, jnp.zeros((nsp - ncp4, nq), F32)], axis=0)
    blk = lax.broadcasted_iota(jnp.int32, (nsp, nq), 0)
    cur = jnp.right_shift(lane_t, 6)
    forced = (blk == 0) | (blk == cur) | (blk == cur - 1)
    w = jnp.where(forced, jnp.inf, jnp.where(blk <= cur, pslc, -jnp.inf))
    alive = _top_rows(w, blk, SEL_TOP)
    sel_s[...] = jnp.where(alive == BIG_I, jnp.where(blk <= cur, 1.0, 0.0), 0.0)


def _attn_kernel(q_ref, ng_ref, kc_ref, kct_ref, cpos_ref, ks_ref, kst_ref, kw_ref, kwt_ref, o_ref,
                 m_s, l_s, acc_s, sel_s, sh_s, gt_s, bias_s, *, q_base, win_base):
    tq, tk = ATT_TQ, ATT_TK
    g = pl.program_id(1)
    t0 = q_base + pl.program_id(2) * tq
    t0f = t0.astype(F32)
    ncp = kc_ref.shape[2]
    slopes = [jnp.where(g == 0, 2.0 ** -(h + 1), 2.0 ** -(GROUP + h + 1)).astype(F32) for h in range(GROUP)]
    hs = [slice(h * tq, (h + 1) * tq) for h in range(GROUP)]
    q = q_ref[0]
    qt = jnp.concatenate([q[:, h * LANE:(h + 1) * LANE].astype(F32).T for h in range(GROUP)], axis=1).astype(BF16)
    lane_t = t0 + lax.broadcasted_iota(jnp.int32, (1, tq), 1)

    sc = _dot(kc_ref[0, 0], qt)
    cpos = cpos_ref[...]
    valid_c = cpos <= lane_t.astype(F32)
    dist = cpos - t0f
    imp = jnp.zeros((ncp, tq), F32)
    pn_parts = []
    for h in range(GROUP):
        s = jnp.where(valid_c, sc[:, hs[h]] + slopes[h] * dist, NEG)
        m = jnp.max(s, axis=0, keepdims=True)
        p = jnp.where(valid_c, jnp.exp(s - m), 0.0)
        pn = p * (1.0 / jnp.maximum(jnp.sum(p, axis=0, keepdims=True), 1e-30))
        imp = imp + pn
        pn_parts.append(pn.astype(BF16))
    oc = _dot(kct_ref[0, 0], jnp.concatenate(pn_parts, axis=1))

    _select_blocks(imp, sh_s, sel_s, lane_t)

    m_s[...] = jnp.full(m_s.shape, NEG, F32)
    l_s[...] = jnp.zeros(l_s.shape, F32)
    acc_s[...] = jnp.zeros(acc_s.shape, F32)
    rowf = lax.broadcasted_iota(jnp.int32, (tk, tq), 0).astype(F32)
    for h in range(GROUP):
        bias_s[h] = slopes[h] * rowf
    blocks = tk // SEL_BLOCK

    def key_tile(kt, causal):
        k0 = pl.multiple_of(kt * tk, tk)
        k_tile = ks_ref[0, 0, pl.ds(k0, tk), :]
        kt_tile = kst_ref[0, 0, :, pl.ds(k0, tk)]
        selm = jnp.concatenate(
            [jnp.broadcast_to(sel_s[pl.ds(kt * blocks + j, 1), :], (SEL_BLOCK, tq)) for j in range(blocks)], axis=0)
        if causal:
            dtab = lax.broadcasted_iota(jnp.int32, (tk, tq), 0) - lax.broadcasted_iota(jnp.int32, (tk, tq), 1)
            selm = jnp.where(dtab <= t0 - k0, selm, 0.0)
        valid = selm > 0.5
        shift = (k0 - t0).astype(F32)
        st = _dot(k_tile, qt)
        p_parts, a_parts = [], []
        for h in range(GROUP):
            c = slopes[h] * shift
            s = jnp.where(valid, st[:, hs[h]] + bias_s[h], NEG)
            m_old = m_s[:, hs[h]]
            m_new = jnp.maximum(m_old, jnp.max(s, axis=0, keepdims=True) + c)
            a = jnp.exp(m_old - m_new)
            p = jnp.exp(s - (m_new - c))
            l_s[:, hs[h]] = a * l_s[:, hs[h]] + jnp.sum(p, axis=0, keepdims=True)
            m_s[:, hs[h]] = m_new
            p_parts.append(p.astype(BF16))
            a_parts.append(a)
        acc_s[...] = acc_s[...] * jnp.concatenate(a_parts, axis=1) + _dot(kt_tile, jnp.concatenate(p_parts, axis=1))

    last = (t0 + tq - 1) // tk
    n_tiles = sel_s.shape[0] // blocks
    assert n_tiles <= 32
    tile_any = jnp.max(jnp.max(sel_s[...].reshape(n_tiles, blocks, tq), axis=1), axis=1, keepdims=True)
    tile_bit = jnp.left_shift(tile_any.astype(jnp.int32), lax.broadcasted_iota(jnp.int32, (n_tiles, 1), 0))
    active = jnp.sum(tile_bit)

    def maybe_tile(kt, carry):
        @pl.when(jnp.bitwise_and(jnp.right_shift(active, kt), 1) == 1)
        def _():
            key_tile(kt, False)
        return carry

    lax.fori_loop(0, last, maybe_tile, 0)
    key_tile(last, True)
    osel = acc_s[...] * (1.0 / l_s[...])

    kr0 = pl.multiple_of(jnp.maximum(t0 - WINDOW - win_base, 0), LANE)
    sw = _dot(kw_ref[0, 0, pl.ds(kr0, WIN_ROWS), :], qt)
    wrow = lax.broadcasted_iota(jnp.int32, (WIN_ROWS, tq), 0)
    dw = (lane_t - win_base - kr0) - wrow
    valid_w = (dw >= 0) & (dw < WINDOW)
    relw = wrow.astype(F32) + (win_base + kr0 - t0).astype(F32)
    pw_parts = []
    for h in range(GROUP):
        s = jnp.where(valid_w, sw[:, hs[h]] + slopes[h] * relw, NEG)
        m = jnp.max(s, axis=0, keepdims=True)
        p = jnp.where(valid_w, jnp.exp(s - m), 0.0)
        pn = p * (1.0 / jnp.maximum(jnp.sum(p, axis=0, keepdims=True), 1e-30))
        pw_parts.append(pn.astype(BF16))
    ow = _dot(kwt_ref[0, 0, :, pl.ds(kr0, WIN_ROWS)], jnp.concatenate(pw_parts, axis=1))

    gt_s[...] = ng_ref[0].T
    o_parts = []
    for h in range(GROUP):
        grow = 3 * (GROUP * g + h)
        o_parts.append(gt_s[pl.ds(grow, 1), :] * oc[HEAD_DIM:, hs[h]]
                       + gt_s[pl.ds(grow + 1, 1), :] * osel[HEAD_DIM:, hs[h]]
                       + gt_s[pl.ds(grow + 2, 1), :] * ow[HEAD_DIM:, hs[h]])
    o_ref[0] = jnp.concatenate(o_parts, axis=0).T.astype(o_ref.dtype)


def _cmp_positions(n_cmp_rows, n_valid):
    p = np.arange(n_cmp_rows)
    c = RATIO * (p % (n_cmp_rows // RATIO)) + p // (n_cmp_rows // RATIO)
    pos = np.where(c < n_valid, c * CMP_STRIDE + CMP_BLOCK - 1, 1e9).astype(np.float32)
    return jnp.asarray(np.broadcast_to(pos[:, None], (n_cmp_rows, LANE)))


def _ratio_major(kc):
    nb, ncp, c = kc.shape
    return kc.reshape(nb, ncp // RATIO, RATIO, c).transpose(0, 2, 1, 3).reshape(nb, ncp, c)


def _attention(q, ng, kc, ks, kw, *, n_cmp_valid, q_base, win_base):
    nb, lq, _ = q.shape
    ncp, tkp, wr = kc.shape[1], ks.shape[1], kw.shape[1]
    nsp = tkp // SEL_BLOCK
    heads = lambda x: x.reshape(x.shape[0], x.shape[1], N_KV, LANE).transpose(0, 2, 1, 3)
    kc, ks, kw = heads(_ratio_major(kc)), heads(ks), heads(kw)
    tr = lambda x: jnp.swapaxes(x, 2, 3)
    cpos = _cmp_positions(ncp, n_cmp_valid)
    r = GROUP * ATT_TQ
    res = lambda rows, cols: pl.BlockSpec((1, 1, rows, cols), lambda b, g, i: (b, g, 0, 0), pipeline_mode=pl.Buffered(1))
    return pl.pallas_call(
        functools.partial(_attn_kernel, q_base=q_base, win_base=win_base),
        grid=(nb, N_KV, lq // ATT_TQ),
        in_specs=[pl.BlockSpec((1, ATT_TQ, GROUP * LANE), lambda b, g, i: (b, i, g)),
                  pl.BlockSpec((1, ATT_TQ, LANE), lambda b, g, i: (b, i, 0)),
                  res(ncp, LANE), res(LANE, ncp),
                  pl.BlockSpec((ncp, LANE), lambda b, g, i: (0, 0), pipeline_mode=pl.Buffered(1)),
                  res(tkp, LANE), res(LANE, tkp), res(wr, LANE), res(LANE, wr)],
        out_specs=pl.BlockSpec((1, ATT_TQ, GROUP * HEAD_DIM), lambda b, g, i: (b, i, g)),
        out_shape=jax.ShapeDtypeStruct((nb, lq, Q_W), BF16),
        scratch_shapes=[pltpu.VMEM((1, r), F32), pltpu.VMEM((1, r), F32), pltpu.VMEM((LANE, r), F32),
                        pltpu.VMEM((nsp, ATT_TQ), F32), pltpu.VMEM((ncp // RATIO + SUBLANE, ATT_TQ), F32),
                        pltpu.VMEM((LANE, ATT_TQ), F32), pltpu.VMEM((GROUP, ATT_TK, ATT_TQ), F32)],
        compiler_params=_cparams("arbitrary", "arbitrary", "arbitrary"),
        name="nsa_attention",
    )(q, ng, kc, tr(kc), cpos, ks, tr(ks), kw, tr(kw))


def _sample_attn_kernel(pt_ref, cache_ref, q2_ref, q4_ref, gate_ref, lt_ref, slope_ref, gsum_ref, kc_ref, kct_ref,
                        cpos_ref, win_ref, ksn_ref, vsn_ref, kwn_ref, vwn_ref, o_ref,
                        pbuf, m_s, l_s, acc_s, sel_s, sh_s, sem, *, n_pages, n_q):
    past = n_pages * PAGE_SIZE
    slot = _pages_pipeline(cache_ref, pt_ref, pbuf, sem, n_pages)
    lane_t = lt_ref[0:1, :]
    tf = lane_t.astype(F32)
    tq = lane_t - past
    slope = slope_ref[0:1, :]
    q2 = q2_ref[0]

    def softmax_update(s_tiles, v_tiles):
        m_old = m_s[...]
        m_new = m_old
        for s in s_tiles:
            m_new = jnp.maximum(m_new, jnp.max(s, axis=0, keepdims=True))
        a = jnp.exp(m_old - m_new)
        l_new = a * l_s[...]
        acc = acc_s[...] * a
        for s, v in zip(s_tiles, v_tiles):
            p = jnp.exp(s - m_new)
            l_new = l_new + jnp.sum(p, axis=0, keepdims=True)
            acc = acc + _dot(v, p.astype(BF16))
        l_s[...] = l_new
        m_s[...] = m_new
        acc_s[...] = acc

    cpos = cpos_ref[...]
    valid_c = cpos <= tf
    s = jnp.where(valid_c, _dot(kc_ref[0], q4_ref[0]) + slope * (cpos - tf), NEG)
    m = jnp.max(s, axis=0, keepdims=True)
    p = jnp.where(valid_c, jnp.exp(s - m), 0.0)
    pn = p * (1.0 / jnp.maximum(jnp.sum(p, axis=0, keepdims=True), 1e-30))
    oc = _dot(kct_ref[0], pn.astype(BF16))
    imp = jnp.dot(pn, gsum_ref[...], preferred_element_type=F32, precision=lax.Precision.HIGHEST)
    _select_blocks(imp, sh_s, sel_s, lane_t)

    m_s[...] = jnp.full(m_s.shape, NEG, F32)
    l_s[...] = jnp.zeros(l_s.shape, F32)
    acc_s[...] = jnp.zeros(acc_s.shape, F32)
    rowi = lax.broadcasted_iota(jnp.int32, (PAGE_SIZE, LANE), 0)
    rowf = rowi.astype(F32)
    blocks = PAGE_SIZE // SEL_BLOCK

    def page_group(i, carry):
        p0 = i * PAGE_UNROLL

        @pl.when(jnp.max(sel_s[pl.ds(p0 * blocks, PAGE_UNROLL * blocks), :]) > 0.5)
        def _():
            s_tiles, v_tiles = [], []
            for j in range(PAGE_UNROLL):
                p = p0 + j
                st = lax.dot_general(pbuf[slot, p, 0:KV_W, :].astype(BF16), q2, TN_DIMS, preferred_element_type=F32)
                selm = jnp.concatenate(
                    [jnp.broadcast_to(sel_s[pl.ds(p * blocks + k, 1), :], (SEL_BLOCK, LANE)) for k in range(blocks)],
                    axis=0)
                pos = rowf + lax.convert_element_type(p * PAGE_SIZE, F32)
                s_tiles.append(jnp.where(selm > 0.5, st + slope * (pos - tf), NEG))
                v_tiles.append(pbuf[slot, p, KV_W:PAGE_ROWS, :].astype(BF16))
            softmax_update(s_tiles, v_tiles)
        return carry

    lax.fori_loop(0, n_pages // PAGE_UNROLL, page_group, 0)
    new_ok = rowi <= tq
    cur_sel = sel_s[pl.ds(n_pages * blocks, 1), :] > 0.5
    s = jnp.where(new_ok, jnp.where(cur_sel, _dot(ksn_ref[0], q2) + slope * (rowf + (past - tf)), NEG), NEG)
    softmax_update([s], [vsn_ref[0]])
    osel = acc_s[...] * (1.0 / jnp.maximum(l_s[...], 1e-30))

    nbuf = win_ref.shape[2]
    wrow = lax.broadcasted_iota(jnp.int32, (nbuf, LANE), 0)
    sw = lax.dot_general(win_ref[0, 0:KV_W, :].astype(BF16), q2, TN_DIMS, preferred_element_type=F32)
    dw = (tq + nbuf) - wrow
    s_old = jnp.where(dw < WINDOW, sw - slope * dw.astype(F32), NEG)
    s_new = jnp.where(new_ok, _dot(kwn_ref[0], q2) + slope * (rowf + (past - tf)), NEG)
    m = jnp.maximum(jnp.max(s_old, axis=0, keepdims=True), jnp.max(s_new, axis=0, keepdims=True))
    p_old = jnp.exp(s_old - m)
    p_new = jnp.exp(s_new - m)
    inv = 1.0 / (jnp.sum(p_old, axis=0, keepdims=True) + jnp.sum(p_new, axis=0, keepdims=True))
    ow = (_dot(win_ref[0, KV_W:PAGE_ROWS, :].astype(BF16), (p_old * inv).astype(BF16))
          + _dot(vwn_ref[0], (p_new * inv).astype(BF16)))

    head0 = lax.broadcasted_iota(jnp.int32, (HEAD_DIM, LANE), 1) < GROUP * n_q
    pick = lambda x, off: jnp.where(head0, x[off:off + HEAD_DIM], x[off + x.shape[0] // 2:off + x.shape[0] // 2 + HEAD_DIM])
    o_ref[0] = (gate_ref[0, 0:1, :] * pick(oc, HEAD_DIM) + gate_ref[0, 1:2, :] * pick(osel, 0)
                + gate_ref[0, 2:3, :] * pick(ow, 0))


def _sample_attention(q, ng, kv_new, kc, cache_sel, win_state, page_table):
    nb, lq, _ = q.shape
    n_pages = page_table.shape[1]
    past = n_pages * PAGE_SIZE
    ncp = kc.shape[1]
    nbuf = win_state.shape[2]
    lanes = N_KV * GROUP * lq
    assert lanes <= LANE // 2 and nbuf == WINDOW and lq <= SUBLANE
    nsp = -(-(past // SEL_BLOCK + 1) // SUBLANE) * SUBLANE

    qh = q.reshape(nb, lq, N_KV, GROUP, 2, HEAD_DIM)[..., 0, :].astype(F32).transpose(0, 2, 3, 1, 4)
    eye = jnp.eye(N_KV, dtype=F32)
    lane_pad = lambda x: jnp.pad(x.reshape(nb, -1, lanes), ((0, 0), (0, 0), (0, LANE - lanes))).astype(BF16)
    q2 = lane_pad(jnp.einsum('bghtd,gk->bkdght', qh, eye))
    q4 = lane_pad(jnp.einsum('bghtd,gk,s->bksdght', qh, eye, jnp.array([1.0, 0.0], F32)))
    gates = ng[:, :, :3 * N_HEADS].reshape(nb, lq, N_KV, GROUP, 3).transpose(0, 4, 2, 3, 1).reshape(nb, 3, lanes)
    gates = jnp.pad(gates, ((0, 0), (0, SUBLANE - 3), (0, LANE - lanes)))
    lane = np.arange(LANE)
    real = lane < lanes
    lane_g, lane_h, lane_q = lane // (GROUP * lq), (lane // lq) % GROUP, lane % lq
    lt = jnp.asarray(np.broadcast_to(np.where(real, past + lane_q, past).astype(np.int32), (SUBLANE, LANE)))
    slope = jnp.asarray(np.broadcast_to(np.where(real, 2.0 ** -(GROUP * lane_g + lane_h + 1.0), 1.0)
                                        .astype(np.float32), (SUBLANE, LANE)))
    same = (lane_g[:, None] == lane_g[None, :]) & (lane_q[:, None] == lane_q[None, :]) & real[:, None] & real[None, :]
    gsum = jnp.asarray(same.astype(np.float32))
    kcp = _ratio_major(kc)
    cpos = _cmp_positions(ncp, past // CMP_STRIDE - 1)
    new_rows = lambda c: jnp.pad(kv_new[:, :, c * KV_W:(c + 1) * KV_W], ((0, 0), (0, LANE - lq), (0, 0))).astype(BF16)
    ksn, vsn, kwn, vwn = new_rows(2), jnp.swapaxes(new_rows(3), 1, 2), new_rows(4), jnp.swapaxes(new_rows(5), 1, 2)

    per_b = lambda *shape: pl.BlockSpec((1,) + shape, lambda b, pt: (b,) + (0,) * len(shape))
    const = lambda arr: pl.BlockSpec(arr.shape, lambda b, pt: (0,) * arr.ndim, pipeline_mode=pl.Buffered(1))
    o = pl.pallas_call(
        functools.partial(_sample_attn_kernel, n_pages=n_pages, n_q=lq),
        grid_spec=pltpu.PrefetchScalarGridSpec(
            num_scalar_prefetch=1,
            grid=(nb,),
            in_specs=[pl.BlockSpec(memory_space=pl.ANY),
                      per_b(KV_W, LANE), per_b(PAGE_ROWS, LANE), per_b(SUBLANE, LANE),
                      const(lt), const(slope), const(gsum),
                      per_b(ncp, PAGE_ROWS), per_b(PAGE_ROWS, ncp), const(cpos),
                      per_b(PAGE_ROWS, nbuf), per_b(LANE, KV_W), per_b(KV_W, LANE), per_b(LANE, KV_W), per_b(KV_W, LANE)],
            out_specs=per_b(HEAD_DIM, LANE),
            scratch_shapes=[pltpu.VMEM((2, n_pages, PAGE_ROWS, PAGE_SIZE), F32),
                            pltpu.VMEM((1, LANE), F32), pltpu.VMEM((1, LANE), F32), pltpu.VMEM((KV_W, LANE), F32),
                            pltpu.VMEM((nsp, LANE), F32), pltpu.VMEM((ncp // RATIO + SUBLANE, LANE), F32),
                            pltpu.SemaphoreType.DMA((2,))]),
        out_shape=jax.ShapeDtypeStruct((nb, HEAD_DIM, LANE), F32),
        compiler_params=_cparams("arbitrary"),
        name="sample_attention",
    )(page_table, cache_sel, q2, q4, gates, lt, slope, gsum, kcp, jnp.swapaxes(kcp, 1, 2), cpos,
      win_state, ksn, vsn, kwn, vwn)
    o = o[:, :, :lanes].reshape(nb, HEAD_DIM, N_KV, GROUP, lq).transpose(0, 4, 2, 3, 1)
    return o.reshape(nb, lq, Q_W).astype(BF16)


HIST = 32


def _mix_kernel(x_ref, a_ref, hist_ref, o_ref, ga_ref, gb_ref, g1_ref, dw_ref, dwb_ref, cg_ref, cb_ref,
                wco_ref, wno_ref, wout_ref, l1g_ref, l1b_ref, x1_ref, ext_s, *, zero_first_hist):
    sb, lb, d = x_ref.shape
    tm = sb * lb
    hist = hist_ref[...]
    if zero_first_hist:
        hist = jnp.where(pl.program_id(1) > 0, hist, 0.0)
    ext_s[:, 0:HIST, :] = hist
    ext_s[:, HIST:HIST + lb, :] = a_ref[...]
    y = jnp.zeros((sb, lb, D_CONV), F32) + dwb_ref[...]
    for j in range(CONV_W):
        y = y + dw_ref[j:j + 1, :] * ext_s[:, pl.ds(j + HIST - (CONV_W - 1), lb), :]
    yn = _ln(y) * cg_ref[...] + cb_ref[...]
    act = (yn * _sigmoid(yn)).reshape(tm, D_CONV)
    out_a = _dot(act.astype(BF16), wco_ref[...])
    out_b = _dot(o_ref[...].reshape(tm, Q_W), wno_ref[...])
    merged = (ga_ref[...].reshape(tm, d).astype(F32) * out_a + gb_ref[...].reshape(tm, d).astype(F32) * out_b)
    mix = _dot(merged.astype(BF16), wout_ref[...]).reshape(sb, lb, d)
    x1_ref[...] = _ln(ALPHA * x_ref[...] + g1_ref[...] * mix) * l1g_ref[...] + l1b_ref[...]


def _mix(x, a, hist, o, ga, gb, mod, lw, tm, *, hist_from_a):
    nseq, seqlen, d = x.shape
    sb, lb, grid = _token_blocks(nseq, seqlen, tm)
    tok = lambda width: pl.BlockSpec((sb, lb, width), lambda s, t: (s, t, 0))
    if hist_from_a:
        hist_spec = pl.BlockSpec((sb, HIST, D_CONV), lambda s, t: (s, jnp.maximum(t * (lb // HIST) - 1, 0), 0))
    else:
        hist_spec = pl.BlockSpec((sb, HIST, D_CONV), lambda s, t: (s, 0, 0))
    const = lambda arr: pl.BlockSpec(arr.shape, lambda s, t: (0,) * arr.ndim, pipeline_mode=pl.Buffered(1))
    return pl.pallas_call(
        functools.partial(_mix_kernel, zero_first_hist=hist_from_a),
        grid=grid,
        in_specs=[tok(d), tok(D_CONV), hist_spec, tok(Q_W), tok(d), tok(d),
                  pl.BlockSpec((sb, 1, d), lambda s, t: (s, 0, 2))] + [const(w) for w in lw],
        out_specs=tok(d),
        out_shape=jax.ShapeDtypeStruct((nseq, seqlen, d), F32),
        scratch_shapes=[pltpu.VMEM((sb, HIST + lb, D_CONV), F32)],
        compiler_params=_cparams("parallel", "arbitrary"),
        name="mix",
    )(x, a, hist, o, ga, gb, mod, *lw)


NO_RANK = 99
CAND_ROWS = PEER_TOPK + SUBLANE * (PEER_TOPK - 1)


def _top_ranked(w, idx, n_top, tie_order):
    rank = jnp.full(w.shape, NO_RANK, jnp.int32)
    vals = []
    for k in range(n_top):
        m = jnp.max(w, axis=0, keepdims=True)
        hit = w == m
        if tie_order:
            hit = idx == jnp.min(jnp.where(hit, idx, BIG_I), axis=0, keepdims=True)
        w = jnp.where(hit, -jnp.inf, w)
        rank = jnp.where(hit, k, rank)
        vals.append(m)
    return rank, vals


def _peer_sel_kernel(x1_ref, sh_ref, sc_ref, wq_ref, keys_ref, h2t_ref, n0_ref, e0_ref, r1_ref, e1_ref, q_s):
    sb, lb, d = x1_ref.shape
    tm = sb * lb
    h2 = (_ln(x1_ref[...]) * (1.0 + sc_ref[...]) + sh_ref[...]).reshape(tm, d)
    h2t_ref[...] = h2.T.astype(BF16)
    q_s[...] = _dot(h2.astype(BF16), wq_ref[...]).astype(BF16)
    idx = lax.broadcasted_iota(jnp.int32, (N_KEYS, tm), 0)
    row8 = lax.broadcasted_iota(jnp.int32, (SUBLANE, tm), 0)
    cidx = lax.broadcasted_iota(jnp.int32, (CAND_ROWS, tm), 0)

    def retrieve(h, s0, s1, tie_order):
        rank0, v0 = _top_ranked(s0, idx, PEER_TOPK, tie_order)
        rank1, v1 = _top_ranked(s1, idx, PEER_TOPK, tie_order)
        v1a = jnp.concatenate(v1[:SUBLANE], axis=0)
        v1b = jnp.concatenate(v1[SUBLANE:], axis=0)
        pieces = [v1a + v0[0], v1b + v0[0]]
        for i in range(1, PEER_TOPK):
            pieces.append(jnp.where(row8 < PEER_TOPK // (i + 1), v1a + v0[i], -jnp.inf))
        cand = jnp.concatenate(pieces, axis=0)
        crank, _ = _top_ranked(cand, cidx, PEER_TOPK, tie_order)
        chosen = crank < PEER_TOPK
        top = v0[0] + v1[0]
        z = jnp.sum(jnp.where(chosen, jnp.exp(cand - top), 0.0), axis=0, keepdims=True)
        cf = jnp.where(chosen, 1.0, 0.0)
        counts = [jnp.sum(cf[0:2 * SUBLANE], axis=0, keepdims=True)]
        for i in range(1, PEER_TOPK):
            counts.append(jnp.sum(cf[SUBLANE * (i + 1):SUBLANE * (i + 2)], axis=0, keepdims=True))
        n0 = jnp.zeros((N_KEYS, tm), F32)
        for i in range(PEER_TOPK):
            n0 = jnp.where(rank0 == i, counts[i], n0)
        n0_ref[h] = n0
        e0_ref[h] = jnp.where(rank0 < PEER_TOPK, jnp.exp(s0 - v0[0]), 0.0)
        r1_ref[h] = rank1.astype(F32).astype(BF16)
        e1_ref[h] = (jnp.where(rank1 < PEER_TOPK, jnp.exp(s1 - v1[0]), 0.0) * (1.0 / z)).astype(BF16)
        ranked = lambda r: jnp.sum(jnp.where(r < PEER_TOPK, 1.0, 0.0), axis=0, keepdims=True)
        return ranked(rank0) + ranked(rank1) + ranked(crank)

    def head(h, carry):
        s0 = lax.dot_general(keys_ref[2 * h], q_s[:, pl.ds(pl.multiple_of(2 * h * N_KEYS, N_KEYS), N_KEYS)], NT_DIMS,
                             preferred_element_type=F32)
        s1 = lax.dot_general(keys_ref[2 * h + 1], q_s[:, pl.ds(pl.multiple_of((2 * h + 1) * N_KEYS, N_KEYS), N_KEYS)],
                             NT_DIMS, preferred_element_type=F32)
        n_ranked = retrieve(h, s0, s1, False)

        @pl.when(jnp.max(n_ranked) > 3.0 * PEER_TOPK)
        def _():
            retrieve(h, s0, s1, True)
        return carry

    lax.fori_loop(0, PEER_HEADS, head, 0)


def _peer_sel(x1, mod, wq, keys, tm):
    nseq, seqlen, d = x1.shape
    n = nseq * seqlen
    sb, lb, grid = _token_blocks(nseq, seqlen, tm)
    tm = sb * lb
    tok = pl.BlockSpec((sb, lb, d), lambda s, t: (s, t, 0))
    flat = lambda s, t: s * grid[1] + t
    stat = pl.BlockSpec((PEER_HEADS, N_KEYS, tm), lambda s, t: (0, 0, flat(s, t)))
    const = lambda arr: pl.BlockSpec(arr.shape, lambda s, t: (0,) * arr.ndim, pipeline_mode=pl.Buffered(1))
    stat_shape = jax.ShapeDtypeStruct((PEER_HEADS, N_KEYS, n), F32)
    slab_shape = jax.ShapeDtypeStruct((PEER_HEADS, N_KEYS, n), BF16)
    return pl.pallas_call(
        _peer_sel_kernel,
        grid=grid,
        in_specs=[tok, pl.BlockSpec((sb, 1, d), lambda s, t: (s, 0, 3)), pl.BlockSpec((sb, 1, d), lambda s, t: (s, 0, 4)),
                  const(wq), const(keys)],
        out_specs=[pl.BlockSpec((d, tm), lambda s, t: (0, flat(s, t))), stat, stat, stat, stat],
        out_shape=[jax.ShapeDtypeStruct((d, n), BF16), stat_shape, stat_shape, slab_shape, slab_shape],
        scratch_shapes=[pltpu.VMEM((tm, 2 * PEER_HEADS * N_KEYS), BF16)],
        compiler_params=_cparams("parallel", "parallel"),
        name="peer_select",
    )(x1, mod, mod, wq, keys)


PEER_CE = 2048
PEER_PART = 1024


def _peer_ffn_kernel(h2t_ref, u_ref, vt_ref, n0_ref, e0_ref, r1_ref, e1_ref, x1_ref, g2_ref, l2g_ref, l2b_ref,
                     y_ref, acc_s, wg_s):
    c = pl.program_id(2)

    @pl.when(c == 0)
    def _():
        acc_s[...] = jnp.zeros(acc_s.shape, F32)

    tm = h2t_ref.shape[1]
    acc = acc_s[...]
    for part in range(PEER_CE // PEER_PART):
        prow = slice(part * PEER_PART, (part + 1) * PEER_PART)
        ut = _dot(u_ref[prow, :], h2t_ref[...])
        for el in range(PEER_PART // N_KEYS):
            e = part * (PEER_PART // N_KEYS) + el
            w = jnp.zeros((N_KEYS, tm), BF16)
            for h in range(PEER_HEADS):
                slab = lambda ref: jnp.tile(jnp.broadcast_to(ref[h, e:e + 1, :], (2 * SUBLANE, tm)).astype(BF16),
                                            (N_KEYS // (2 * SUBLANE), 1))
                w = w + jnp.where(r1_ref[h] < slab(n0_ref), e1_ref[h], jnp.zeros((), BF16)) * slab(e0_ref)
            wg_s[e * N_KEYS:(e + 1) * N_KEYS, :] = w * _gelu(ut[el * N_KEYS:(el + 1) * N_KEYS]).astype(BF16)
        acc = acc + _dot(vt_ref[:, prow], wg_s[prow, :])
    acc_s[...] = acc

    @pl.when(c == pl.num_programs(2) - 1)
    def _():
        sb, lb, d = x1_ref.shape
        f = acc_s[...].T.reshape(sb, lb, d)
        y_ref[...] = _ln(ALPHA * x1_ref[...] + g2_ref[...] * f) * l2g_ref[...] + l2b_ref[...]


def _peer_ffn(h2t, u, vt, stats, x1, mod, l2g, l2b, tm):
    nseq, seqlen, d = x1.shape
    sb, lb, grid = _token_blocks(nseq, seqlen, tm)
    tm = sb * lb
    n_exp = u.shape[0]
    flat = lambda s, t: s * grid[1] + t
    tok = pl.BlockSpec((sb, lb, d), lambda s, t, c: (s, t, 0))
    c1 = PEER_CE // N_KEYS
    row_stat = pl.BlockSpec((PEER_HEADS, c1, tm), lambda s, t, c: (0, c, flat(s, t)))
    slab_stat = pl.BlockSpec((PEER_HEADS, N_KEYS, tm), lambda s, t, c: (0, 0, flat(s, t)))
    vec = pl.BlockSpec((1, d), lambda s, t, c: (0, 0))
    n0, e0, r1, e1 = stats
    return pl.pallas_call(
        _peer_ffn_kernel,
        grid=grid + (n_exp // PEER_CE,),
        in_specs=[pl.BlockSpec((d, tm), lambda s, t, c: (0, flat(s, t))),
                  pl.BlockSpec((PEER_CE, d), lambda s, t, c: (c, 0)),
                  pl.BlockSpec((d, PEER_CE), lambda s, t, c: (0, c)),
                  row_stat, row_stat, slab_stat, slab_stat, tok,
                  pl.BlockSpec((sb, 1, d), lambda s, t, c: (s, 0, 5)), vec, vec],
        out_specs=tok,
        out_shape=jax.ShapeDtypeStruct((nseq, seqlen, d), F32),
        scratch_shapes=[pltpu.VMEM((d, tm), F32), pltpu.VMEM((PEER_CE, tm), BF16)],
        compiler_params=_cparams("parallel", "parallel", "arbitrary"),
        name="peer_experts",
    )(h2t, u, vt, n0, e0, r1, e1, x1, mod, l2g, l2b)


TOKEN_TILE = 512
PEER_SEL_TILE = 256


def _channel_mixing(x, mod, a, o, ga, gb, hist, lw_mix, peer, *, hist_from_a):
    x1 = _mix(x, a, hist, o, ga, gb, mod, lw_mix, TOKEN_TILE, hist_from_a=hist_from_a)
    wq, keys, u, vt, l2g, l2b = peer
    h2t, *stats = _peer_sel(x1, mod, wq, keys, PEER_SEL_TILE)
    return _peer_ffn(h2t, u, vt, stats, x1, mod, l2g, l2b, TOKEN_TILE)


def kernel(x_prompt, x_sample, cache_cmp_kv, cache_sel_kv, state_win_kv, state_conv, page_table, c_prompt, c_sample,
           w_ada, b_ada, w_in, b_in, conv_dw, conv_dw_b, conv_ln_g, conv_ln_b, w_conv_out, cmp_pos, w_cmp_k1, w_cmp_k2,
           w_cmp_v1, w_cmp_v2, w_nsa_out, w_out, ln1_g, ln1_b, peer_wq, peer_keys, peer_u, peer_v, ln2_g, ln2_b):
    assert w_ada.shape[0] == DEPTH == 1
    l = 0
    nbp, s, d = x_prompt.shape
    nbs, ls, _ = x_sample.shape
    n_pages = page_table.shape[1]
    past = n_pages * PAGE_SIZE
    nbuf = state_win_kv.shape[2]
    n_phys = cache_cmp_kv.shape[1]
    assert s % (CMP_ROWS * CMP_STRIDE) == 0 and past % (CMP_ROWS * CMP_STRIDE) == 0 and nbuf == WINDOW
    assert n_pages % PAGE_UNROLL == 0

    w_ext, b_ext = _extend_w_in(w_in[l], b_in[l])
    cw = _compress_weights(cmp_pos[l], w_cmp_k1[l], w_cmp_k2[l], w_cmp_v1[l], w_cmp_v2[l])
    row = lambda v: v.reshape(1, -1).astype(F32)
    lw_mix = (jnp.pad(conv_dw[l], ((0, HIST - CONV_W), (0, 0))), row(conv_dw_b[l]), row(conv_ln_g[l]), row(conv_ln_b[l]),
              w_conv_out[l].astype(BF16), w_nsa_out[l].astype(BF16), w_out[l].astype(BF16), row(ln1_g[l]), row(ln1_b[l]))
    peer = (peer_wq[l].astype(BF16), peer_keys[l].reshape(2 * PEER_HEADS, N_KEYS, -1).astype(BF16),
            peer_u[l].astype(BF16), peer_v[l].T.astype(BF16), row(ln2_g[l]), row(ln2_b[l]))

    c_all = jnp.concatenate([c_prompt, c_sample], axis=0)
    c_all = jnp.pad(c_all, ((0, (-c_all.shape[0]) % SUBLANE), (0, 0)))
    mod = _adaln(c_all, w_ada[l], b_ada[l])
    mod_p = mod[:nbp].reshape(nbp, 1, -1)
    mod_s = mod[nbp:nbp + nbs].reshape(nbs, 1, -1)

    a_p, q_p, kv_p, kvg_p, ng_p, ga_p, gb_p = _inproj(x_prompt, mod_p, w_ext, b_ext, TOKEN_TILE)
    kc_p = _compress_prompt(kv_p[:, :, :2 * KV_W].reshape(nbp, 2 * s, LANE), cw)
    o_p = _attention(q_p, ng_p, kc_p, kvg_p[:, :, 2 * KV_W:4 * KV_W], kvg_p[:, :, 4 * KV_W:],
                     n_cmp_valid=s // CMP_STRIDE - 1, q_base=0, win_base=0)
    y_p = _channel_mixing(x_prompt, mod_p, a_p, o_p, ga_p, gb_p, a_p, lw_mix, peer, hist_from_a=True)

    a_s, q_s, kv_s, kvg_s, ng_s, ga_s, gb_s = _inproj(x_sample, mod_s, w_ext, b_ext, TOKEN_TILE)
    pages = lambda cache: cache[l].transpose(0, 2, 3, 4, 1).reshape(n_phys, PAGE_ROWS, PAGE_SIZE)
    kc_s = _compress_sample(pages(cache_cmp_kv), page_table, cw)
    win_t = state_win_kv[l].transpose(0, 2, 3, 4, 1).reshape(nbs, PAGE_ROWS, nbuf)
    o_s = _sample_attention(q_s, ng_s, kv_s, kc_s, pages(cache_sel_kv), win_t, page_table)
    hist_s = jnp.pad(state_conv[l], ((0, 0), (HIST - (CONV_W - 1), 0), (0, 0)))
    y_s = _channel_mixing(x_sample, mod_s, a_s, o_s, ga_s, gb_s, hist_s, lw_mix, peer, hist_from_a=False)

    kv6 = lambda t, c: t[:, :, c * 2 * KV_W:(c + 1) * 2 * KV_W].reshape(1, t.shape[0], t.shape[1], 2, N_KV, HEAD_DIM)
    nb = min(WINDOW, s)
    win_new = jnp.concatenate([state_win_kv[l], kv6(kv_s, 2)[0]], axis=1)[:, ls:]
    conv_new = jnp.concatenate([state_conv[l], a_s], axis=1)[:, ls:]
    return (y_p, y_s, kv6(kv_p, 0), kv6(kv_p, 1), kv6(kv_p, 2)[:, :, s - nb:], a_p[None, :, s - (CONV_W - 1):],
            kv6(kv_s, 0), kv6(kv_s, 1), win_new[None], conv_new[None])
```
